```python
import jax, jax.numpy as jnp
from jax import lax
import numpy as np

D_MODEL = 2048
BATCH = 1
SEQ = 16384
DEPTH = 1

CHUNK = 64
EPS = 1e-6
CONV_WIDTH = D_MODEL // 2
CONV_GROUPS = 16
CONV_K = 3
SSD_WIDTH = D_MODEL // 2
SSD_HEAD_DIM = 64
SSD_HEADS = SSD_WIDTH // SSD_HEAD_DIM
SSD_GROUPS = 2
SSD_HEADS_PER_GROUP = SSD_HEADS // SSD_GROUPS
SSD_STATE = 128
SSD_CONV_K = 4
SSD_XBC = SSD_WIDTH + 2 * SSD_GROUPS * SSD_STATE
MIX_WIDTH = CONV_WIDTH + SSD_WIDTH
IN_COLS = 3 * CONV_WIDTH + SSD_WIDTH + SSD_XBC + SSD_HEADS
N_EXPERT_GROUPS = 4
EXPERTS_PER_GROUP = 8
N_EXPERTS = N_EXPERT_GROUPS * EXPERTS_PER_GROUP
TOP_K = 2
D_FF_EXPERT = D_MODEL // 2
MOE_BLOCK = 128

kernel_name = "hybrid_conv_ssd_hmoe_block"


def rmsnorm(x, w):
    xf = x.astype(jnp.float32)
    y = xf * lax.rsqrt(jnp.mean(xf * xf, axis=-1, keepdims=True) + EPS)
    return (y * w.astype(jnp.float32)).astype(x.dtype)


def causal_dwconv(x, w):
    k, ch = w.shape
    return lax.conv_general_dilated(
        x, w[:, None, :].astype(x.dtype), window_strides=(1,), padding=[(k - 1, 0)],
        dimension_numbers=('NWC', 'WIO', 'NWC'), feature_group_count=ch)


def short_conv_mixer(u, conv_w):
    b_gate, c_gate, v = jnp.split(u, 3, axis=-1)
    return b_gate * causal_dwconv(c_gate * v, conv_w)


def segsum_exp(a):
    q = a.shape[-1]
    cs = jnp.cumsum(a, axis=-1)
    diff = cs[..., :, None] - cs[..., None, :]
    mask = jnp.tril(jnp.ones((q, q), dtype=bool))
    return jnp.where(mask, jnp.exp(jnp.where(mask, diff, 0.0)), 0.0)


def ssd_mixer(u_z, u_xbc, u_dt, conv_w, conv_b, dt_bias, a_log, d_skip, norm_w):
    bsz, seq_len, _ = u_z.shape
    nc = seq_len // CHUNK
    g, r, p, n = SSD_GROUPS, SSD_HEADS_PER_GROUP, SSD_HEAD_DIM, SSD_STATE
    f32 = jnp.float32
    xbc = jax.nn.silu(causal_dwconv(u_xbc, conv_w) + conv_b).astype(f32)
    xs, bm, cm = jnp.split(xbc, [SSD_WIDTH, SSD_WIDTH + g * n], axis=-1)
    xs = xs.reshape(bsz, nc, CHUNK, g, r, p)
    bm = bm.reshape(bsz, nc, CHUNK, g, n)
    cm = cm.reshape(bsz, nc, CHUNK, g, n)
    dt = jax.nn.softplus(u_dt.astype(f32) + dt_bias.astype(f32))
    a_head = -jnp.exp(a_log.astype(f32))
    dt_r = dt.reshape(bsz, nc, CHUNK, g, r)
    a = (dt * a_head).reshape(bsz, nc, CHUNK, g, r).transpose(0, 3, 4, 1, 2)
    xd = xs * dt_r[..., None]
    a_cs = jnp.cumsum(a, axis=-1)
    cb = jnp.einsum('bclgn,bcsgn->bgcls', cm, bm)
    m = cb[:, :, None] * segsum_exp(a)
    y_diag = jnp.einsum('bgrcls,bcsgrp->bclgrp', m, xd)
    decay_states = jnp.exp(a_cs[..., -1:] - a_cs).transpose(0, 3, 4, 1, 2)
    states = jnp.einsum('bclgn,bclgrp->bcgrpn', bm, xd * decay_states[..., None])
    chunk_decay = jnp.exp(a_cs[..., -1])

    def step(h, inp):
        dec, s = inp
        return h * dec[..., None, None] + s, h

    h0 = jnp.zeros((bsz, g, r, p, n), f32)
    _, prev = lax.scan(step, h0, (jnp.moveaxis(chunk_decay, -1, 0), jnp.moveaxis(states, 1, 0)))
    prev = jnp.moveaxis(prev, 0, 1)
    state_decay = jnp.exp(a_cs).transpose(0, 3, 4, 1, 2)
    y_off = jnp.einsum('bclgn,bcgrpn->bclgrp', cm, prev) * state_decay[..., None]
    y = y_diag + y_off + xs * d_skip.astype(f32).reshape(g, r, 1)
    y = y.reshape(bsz, seq_len, SSD_WIDTH)
    yg = (y * jax.nn.silu(u_z.astype(f32))).reshape(bsz, seq_len, g, SSD_WIDTH // g)
    yg = yg * lax.rsqrt(jnp.mean(yg * yg, axis=-1, keepdims=True) + EPS)
    yg = yg.reshape(bsz, seq_len, SSD_WIDTH) * norm_w.astype(f32)
    return yg.astype(u_z.dtype)


def hier_moe(h, w_rg, b_rg, w_re, b_re, w_gate, w_up, w_down):
    bsz, seq_len, d = h.shape
    t = bsz * seq_len
    ht = h.reshape(t, d)
    grp_prob = jax.nn.softmax((ht @ w_rg + b_rg).astype(jnp.float32), axis=-1)
    grp_p, grp_idx = lax.top_k(grp_prob, 1)
    exp_logits = (ht @ w_re + b_re).astype(jnp.float32).reshape(t, N_EXPERT_GROUPS, EXPERTS_PER_GROUP)
    in_grp = exp_logits[jnp.arange(t), grp_idx[:, 0]]
    top_p, top_local = lax.top_k(jax.nn.softmax(in_grp, axis=-1), TOP_K)
    weights = grp_p * (top_p / jnp.sum(top_p, axis=-1, keepdims=True))
    expert_ids = grp_idx * EXPERTS_PER_GROUP + top_local
    n_assign = t * TOP_K
    e_flat = expert_ids.reshape(n_assign)
    w_flat = weights.reshape(n_assign)
    tok_flat = jnp.repeat(jnp.arange(t, dtype=jnp.int32), TOP_K)
    order = jnp.argsort(e_flat)
    e_sorted = e_flat[order]
    counts = jnp.bincount(e_flat, length=N_EXPERTS)
    padded = ((counts + MOE_BLOCK - 1) // MOE_BLOCK) * MOE_BLOCK
    start = jnp.cumsum(counts) - counts
    padded_end = jnp.cumsum(padded)
    pstart = padded_end - padded
    dest = pstart[e_sorted] + (jnp.arange(n_assign) - start[e_sorted])
    n_blocks = (n_assign + N_EXPERTS * (MOE_BLOCK - 1) + MOE_BLOCK - 1) // MOE_BLOCK
    n_slots = n_blocks * MOE_BLOCK
    slot_tok = jnp.zeros((n_slots,), jnp.int32).at[dest].set(tok_flat[order])
    slot_w = jnp.zeros((n_slots,), jnp.float32).at[dest].set(w_flat[order])
    block_exp = jnp.minimum(
        jnp.searchsorted(padded_end, jnp.arange(n_blocks) * MOE_BLOCK, side='right'), N_EXPERTS - 1)
    tok_blocks = slot_tok.reshape(n_blocks, MOE_BLOCK)
    w_blocks = slot_w.reshape(n_blocks, MOE_BLOCK)

    def run_block(args):
        toks, wts, e = args
        xb = ht[toks]
        hid = jax.nn.silu(xb @ w_gate[e]) * (xb @ w_up[e])
        yb = hid @ w_down[e]
        return (yb * wts[:, None]).astype(h.dtype)

    yb = lax.map(run_block, (tok_blocks, w_blocks, block_exp))
    out = jnp.zeros((t, d), h.dtype).at[slot_tok].add(yb.reshape(n_slots, d))
    return out.reshape(bsz, seq_len, d)


def setup_inputs(seed: int = 0) -> dict:
    key = jax.random.key(seed)
    ks = jax.random.split(key, 24)
    nrm = jax.random.normal
    L, D = DEPTH, D_MODEL
    dt0 = jnp.exp(jax.random.uniform(ks[9], (L, SSD_HEADS), minval=np.log(1e-3), maxval=np.log(1e-1)))
    return {
        "x": nrm(ks[0], (BATCH, SEQ, D), jnp.float32),
        "c": nrm(ks[1], (BATCH, D), jnp.float32),
        "w_ada": nrm(ks[2], (L, D, 6 * D)) * (0.5 * D ** -0.5),
        "b_ada": nrm(ks[3], (L, 6 * D)) * 0.02,
        "norm1_w": 1.0 + 0.02 * nrm(ks[4], (L, D)),
        "w_in": nrm(ks[5], (L, D, IN_COLS)) * D ** -0.5,
        "conv_w": nrm(ks[6], (L, CONV_K, CONV_WIDTH)) * CONV_K ** -0.5,
        "ssd_conv_w": nrm(ks[7], (L, SSD_CONV_K, SSD_XBC)) * SSD_CONV_K ** -0.5,
        "ssd_conv_b": nrm(ks[8], (L, SSD_XBC)) * 0.01,
        "dt_bias": dt0 + jnp.log(-jnp.expm1(-dt0)),
        "a_log": jnp.log(jax.random.uniform(ks[10], (L, SSD_HEADS), minval=1.0, maxval=16.0)),
        "d_skip": 1.0 + 0.02 * nrm(ks[11], (L, SSD_HEADS)),
        "ssd_norm_w": 1.0 + 0.02 * nrm(ks[12], (L, SSD_WIDTH)),
        "w_out": nrm(ks[13], (L, MIX_WIDTH, D)) * MIX_WIDTH ** -0.5,
        "norm2_w": 1.0 + 0.02 * nrm(ks[14], (L, D)),
        "w_router_grp": nrm(ks[15], (L, D, N_EXPERT_GROUPS)) * D ** -0.5,
        "b_router_grp": nrm(ks[16], (L, N_EXPERT_GROUPS)) * 0.01,
        "w_router_exp": nrm(ks[17], (L, D, N_EXPERTS)) * D ** -0.5,
        "b_router_exp": nrm(ks[18], (L, N_EXPERTS)) * 0.01,
        "w_gate": nrm(ks[19], (L, N_EXPERTS, D, D_FF_EXPERT)) * D ** -0.5,
        "w_up": nrm(ks[20], (L, N_EXPERTS, D, D_FF_EXPERT)) * D ** -0.5,
        "w_down": nrm(ks[21], (L, N_EXPERTS, D_FF_EXPERT, D)) * D_FF_EXPERT ** -0.5,
        "final_norm_w": 1.0 + 0.02 * nrm(ks[22], (D,)),
    }


def reference(x, c, w_ada, b_ada, norm1_w, w_in, conv_w, ssd_conv_w, ssd_conv_b, dt_bias,
              a_log, d_skip, ssd_norm_w, w_out, norm2_w, w_router_grp, b_router_grp,
              w_router_exp, b_router_exp, w_gate, w_up, w_down, final_norm_w):
    split_pts = [3 * CONV_WIDTH, 3 * CONV_WIDTH + SSD_WIDTH, 3 * CONV_WIDTH + SSD_WIDTH + SSD_XBC]
    for i in range(DEPTH):
        mod = jax.nn.silu(c) @ w_ada[i] + b_ada[i]
        sh1, sc1, g1, sh2, sc2, g2 = jnp.split(mod[:, None, :], 6, axis=-1)
        h = rmsnorm(x, norm1_w[i]) * (1 + sc1) + sh1
        u = h @ w_in[i]
        u_conv, u_z, u_xbc, u_dt = jnp.split(u, split_pts, axis=-1)
        y_conv = short_conv_mixer(u_conv, conv_w[i])
        y_ssd = ssd_mixer(u_z, u_xbc, u_dt, ssd_conv_w[i], ssd_conv_b[i], dt_bias[i],
                          a_log[i], d_skip[i], ssd_norm_w[i])
        x = x + g1 * (jnp.concatenate([y_conv, y_ssd], axis=-1) @ w_out[i])
        h = rmsnorm(x, norm2_w[i]) * (1 + sc2) + sh2
        x = x + g2 * hier_moe(h, w_router_grp[i], b_router_grp[i], w_router_exp[i],
                              b_router_exp[i], w_gate[i], w_up[i], w_down[i])
    return rmsnorm(x, final_norm_w)
```

```python
import functools

import jax
import jax.numpy as jnp
from jax import lax
from jax.experimental import pallas as pl
from jax.experimental.pallas import tpu as pltpu

F32 = jnp.float32
BF16 = jnp.bfloat16
EPS = 1e-6

CONV_GROUP_W = 1024
SSD_HEADS = 16
SSD_HEAD_DIM = 64
SSD_GROUPS = 2
SSD_STATE = 128
N_EXPERT_GROUPS = 4
EXPERTS_PER_GROUP = 8
N_EXPERTS = 32
LANES = 128
SUBLANES = 8
VMEM_LIMIT = 56 * 1024 * 1024

SSD_CHUNK = 128
MOE_ROWS = 256
HIGHEST = lax.Precision.HIGHEST


def _silu(v):
    return v * (1.0 / (1.0 + jnp.exp(-v)))


def _cparams(sem):
    return pltpu.CompilerParams(dimension_semantics=sem, vmem_limit_bytes=VMEM_LIMIT)


def _ada_kernel(c_ref, w_ref, b_ref, o_ref):
    c = c_ref[...]
    s = _silu(c)
    o_ref[...] = jnp.sum(w_ref[...] * s, axis=0, keepdims=True) + b_ref[...]


def _ada(c, w_ada, b_ada):
    d, n = w_ada.shape
    tn = 1024
    return pl.pallas_call(
        _ada_kernel,
        grid=(n // tn,),
        in_specs=[pl.BlockSpec((d, 1), lambda j: (0, 0)),
                  pl.BlockSpec((d, tn), lambda j: (0, j)),
                  pl.BlockSpec((1, tn), lambda j: (0, j))],
        out_specs=pl.BlockSpec((1, tn), lambda j: (0, j)),
        out_shape=jax.ShapeDtypeStruct((1, n), F32),
        compiler_params=_cparams(("arbitrary",)),
        name="ada",
    )(c.reshape(d, 1), w_ada, b_ada.reshape(1, n))


def _inproj_kernel(x_ref, nw_ref, sc_ref, sh_ref, w_ref, wdt_ref, u_ref, dt_ref, h_ref, *, rows):
    j = pl.program_id(1)
    tm = x_ref.shape[0]

    @pl.when(j == 0)
    def _():
        for r0 in range(0, tm, rows):
            xf = x_ref[r0:r0 + rows, :]
            ms = jnp.mean(xf * xf, axis=-1, keepdims=True)
            y = xf * lax.rsqrt(ms + EPS) * nw_ref[...]
            h = (y * (1.0 + sc_ref[...]) + sh_ref[...]).astype(BF16)
            h_ref[r0:r0 + rows, :] = h
            dt_ref[r0:r0 + rows, :] = jnp.dot(h, wdt_ref[...], preferred_element_type=F32)

    u_ref[...] = jnp.dot(h_ref[...], w_ref[...], preferred_element_type=F32).astype(BF16)


def _inproj(x, nw, sc, sh, w_main, w_dt, tm, tn):
    t, d = x.shape
    n = w_main.shape[1]
    row = lambda i, j: (0, 0)
    return pl.pallas_call(
        functools.partial(_inproj_kernel, rows=256),
        grid=(t // tm, n // tn),
        in_specs=[pl.BlockSpec((tm, d), lambda i, j: (i, 0)),
                  pl.BlockSpec((1, d), row), pl.BlockSpec((1, d), row), pl.BlockSpec((1, d), row),
                  pl.BlockSpec((d, tn), lambda i, j: (0, j)),
                  pl.BlockSpec((d, LANES), row)],
        out_specs=[pl.BlockSpec((tm, tn), lambda i, j: (i, j)),
                   pl.BlockSpec((tm, LANES), lambda i, j: (i, 0))],
        out_shape=[jax.ShapeDtypeStruct((t, n), BF16), jax.ShapeDtypeStruct((t, LANES), F32)],
        scratch_shapes=[pltpu.VMEM((tm, d), BF16)],
        compiler_params=_cparams(("arbitrary", "arbitrary")),
        name="inproj",
    )(x, nw, sc, sh, w_main, w_dt)


def _causal_conv(cur, tail_ref, w_ref, k):
    q = cur.shape[0]
    ext = jnp.concatenate([tail_ref[...], cur], axis=0)
    out = w_ref[k - 1:k, :] * cur
    for s in range(1, k):
        out = out + w_ref[k - 1 - s:k - s, :] * ext[SUBLANES - s:SUBLANES - s + q, :]
    tail_ref[...] = cur[q - SUBLANES:, :]
    return out


def _mix_kernel(ub_ref, uc_ref, uv_ref, uz_ref, uxs_ref, ubc_ref, dt_ref,
                cw_ref, wxs_ref, wbc_ref, bxs_ref, bbc_ref, dtb_ref, alog_ref, dskip_ref,
                nw_ref, eexp_ref, o_ref,
                tail_cv, tail_xs, tail_bc, state_ref):
    i = pl.program_id(0)
    q = ub_ref.shape[0]
    gw = SSD_HEADS // SSD_GROUPS * SSD_HEAD_DIM

    @pl.when(i == 0)
    def _():
        tail_cv[...] = jnp.zeros_like(tail_cv)
        tail_xs[...] = jnp.zeros_like(tail_xs)
        tail_bc[...] = jnp.zeros_like(tail_bc)
        state_ref[...] = jnp.zeros_like(state_ref)

    cv = uc_ref[...].astype(F32) * uv_ref[...].astype(F32)
    y_conv = ub_ref[...].astype(F32) * _causal_conv(cv, tail_cv, cw_ref, 3)
    o_ref[:, :CONV_GROUP_W] = y_conv.astype(BF16)

    xs = _silu(_causal_conv(uxs_ref[...].astype(F32), tail_xs, wxs_ref, 4) + bxs_ref[...])
    bc = _silu(_causal_conv(ubc_ref[...].astype(F32), tail_bc, wbc_ref, 4) + bbc_ref[...])
    gs = SSD_GROUPS * SSD_STATE
    bm, cm = bc[:, :gs], bc[:, gs:]

    dtv = dt_ref[...] + dtb_ref[...]
    dt = jnp.maximum(dtv, 0.0) + jnp.log1p(jnp.exp(-jnp.abs(dtv)))
    a = dt * (-jnp.exp(alog_ref[...]))
    rr = lax.broadcasted_iota(jnp.int32, (q, q), 0)
    cc = lax.broadcasted_iota(jnp.int32, (q, q), 1)
    causal = rr >= cc
    tri = causal.astype(F32)
    a_cs = jnp.dot(tri, a, precision=HIGHEST, preferred_element_type=F32)
    eexp = eexp_ref[...]
    a_cs_x = jnp.dot(a_cs, eexp, precision=HIGHEST, preferred_element_type=F32)
    dt_x = jnp.dot(dt, eexp, precision=HIGHEST, preferred_element_type=F32)
    a_cs_t = a_cs.T

    xd = xs * dt_x
    state_decay = jnp.exp(a_cs_x)
    last = a_cs_x[q - 1:q, :]
    decay_states = jnp.exp(last - a_cs_x)
    chunk_decay = jnp.exp(last)
    xd_b = xd.astype(BF16)
    xds_b = (xd * decay_states).astype(BF16)
    lane = lax.broadcasted_iota(jnp.int32, (q, LANES), 1)
    left = lane < SSD_HEAD_DIM

    y_parts = []
    for g in range(SSD_GROUPS):
        cm_g = cm[:, g * SSD_STATE:(g + 1) * SSD_STATE].astype(BF16)
        bm_g = bm[:, g * SSD_STATE:(g + 1) * SSD_STATE]
        cb = lax.dot_general(cm_g, bm_g.astype(BF16), (((1,), (1,)), ((), ())),
                             preferred_element_type=F32)
        s_prev = state_ref[g]
        y_off = jnp.dot(cm_g, s_prev.astype(BF16), preferred_element_type=F32)
        y_off = y_off * state_decay[:, g * gw:(g + 1) * gw]
        heads_per_group = SSD_HEADS // SSD_GROUPS
        for pr in range(heads_per_group // 2):
            ms = []
            for hh in range(2):
                h = g * heads_per_group + pr * 2 + hh
                diff = a_cs[:, h:h + 1] - a_cs_t[h:h + 1, :]
                l_mat = jnp.where(causal, jnp.exp(jnp.where(causal, diff, 0.0)), 0.0)
                ms.append((cb * l_mat).astype(BF16))
            lhs = jnp.concatenate(ms, axis=1)
            c0 = g * gw + pr * LANES
            xp = xd_b[:, c0:c0 + LANES]
            zero = jnp.zeros_like(xp)
            rhs = jnp.concatenate([jnp.where(left, xp, zero), jnp.where(left, zero, xp)], axis=0)
            y_parts.append(jnp.dot(lhs, rhs, preferred_element_type=F32)
                           + y_off[:, pr * LANES:(pr + 1) * LANES])
        upd = jnp.dot(bm_g.T.astype(BF16), xds_b[:, g * gw:(g + 1) * gw],
                      preferred_element_type=F32)
        state_ref[g] = s_prev * chunk_decay[:, g * gw:(g + 1) * gw] + upd
    y = jnp.concatenate(y_parts, axis=1) + xs * dskip_ref[...]

    yg = y * _silu(uz_ref[...].astype(F32))
    outs = []
    for g in range(SSD_GROUPS):
        v = yg[:, g * gw:(g + 1) * gw]
        ms = jnp.mean(v * v, axis=-1, keepdims=True)
        outs.append(v * lax.rsqrt(ms + EPS))
    y_ssd = jnp.concatenate(outs, axis=1) * nw_ref[...]
    o_ref[:, CONV_GROUP_W:] = y_ssd.astype(BF16)


def _mix(u, dt_raw, conv_w, wxs, wbc, bxs, bbc, dtb, alog, dskip_x, nw, eexp):
    t = u.shape[0]
    q = SSD_CHUNK
    gw = CONV_GROUP_W
    bcw = 2 * SSD_GROUPS * SSD_STATE
    const = lambda i: (0, 0)
    ublk = lambda k: pl.BlockSpec((q, gw), lambda i, k=k: (i, k))
    full = lambda arr: pl.BlockSpec(arr.shape, const)
    return pl.pallas_call(
        _mix_kernel,
        grid=(t // q,),
        in_specs=[ublk(0), ublk(1), ublk(2), ublk(3), ublk(4),
                  pl.BlockSpec((q, bcw), lambda i: (i, 5 * gw // bcw)),
                  pl.BlockSpec((q, LANES), lambda i: (i, 0)),
                  full(conv_w), full(wxs), full(wbc), full(bxs), full(bbc), full(dtb), full(alog),
                  full(dskip_x), full(nw), full(eexp)],
        out_specs=pl.BlockSpec((q, 2 * gw), lambda i: (i, 0)),
        out_shape=jax.ShapeDtypeStruct((t, 2 * gw), BF16),
        scratch_shapes=[pltpu.VMEM((SUBLANES, gw), F32), pltpu.VMEM((SUBLANES, gw), F32),
                        pltpu.VMEM((SUBLANES, bcw), F32),
                        pltpu.VMEM((SSD_GROUPS, SSD_STATE, gw // SSD_GROUPS), F32)],
        compiler_params=_cparams(("arbitrary",)),
        name="mix",
    )(u, u, u, u, u, u, dt_raw, conv_w, wxs, wbc, bxs, bbc, dtb, alog, dskip_x, nw, eexp)


def _outproj_kernel(y_ref, x_ref, w_ref, g1_ref, nw_ref, sc_ref, sh_ref, wr_ref, br_ref,
                    x1_ref, h_ref, route_ref):
    acc = jnp.dot(y_ref[...], w_ref[...], preferred_element_type=F32)
    x1 = x_ref[...] + g1_ref[...] * acc
    x1_ref[...] = x1
    ms = jnp.mean(x1 * x1, axis=-1, keepdims=True)
    h = x1 * lax.rsqrt(ms + EPS) * nw_ref[...] * (1.0 + sc_ref[...]) + sh_ref[...]
    h_ref[...] = h
    logits = jnp.dot(h, wr_ref[...], precision=HIGHEST, preferred_element_type=F32) + br_ref[...]

    tm = logits.shape[0]
    lane = lax.broadcasted_iota(jnp.int32, (tm, LANES), 1)
    ninf = jnp.float32(-jnp.inf)
    big = jnp.int32(1 << 20)
    is_g = (lane >= N_EXPERTS) & (lane < N_EXPERTS + N_EXPERT_GROUPS)
    gl = jnp.where(is_g, logits, ninf)
    gmax = jnp.max(gl, axis=-1, keepdims=True)
    grp_p = 1.0 / jnp.sum(jnp.exp(gl - gmax), axis=-1, keepdims=True)
    gidx = jnp.min(jnp.where(gl == gmax, lane - N_EXPERTS, big), axis=-1, keepdims=True)
    lo = gidx * EXPERTS_PER_GROUP
    el = jnp.where((lane >= lo) & (lane < lo + EXPERTS_PER_GROUP), logits, ninf)
    m1 = jnp.max(el, axis=-1, keepdims=True)
    i1 = jnp.min(jnp.where(el == m1, lane, big), axis=-1, keepdims=True)
    el2 = jnp.where(lane == i1, ninf, el)
    m2 = jnp.max(el2, axis=-1, keepdims=True)
    i2 = jnp.min(jnp.where(el2 == m2, lane, big), axis=-1, keepdims=True)
    e = jnp.exp(m2 - m1)
    w1 = grp_p / (1.0 + e)
    w2 = grp_p * e / (1.0 + e)
    route = jnp.where(lane == 0, i1.astype(F32),
                      jnp.where(lane == 1, i2.astype(F32),
                                jnp.where(lane == 2, w1, jnp.where(lane == 3, w2, 0.0))))
    route_ref[...] = route


def _outproj(ycat, x, w_out, g1, nw, sc, sh, w_r, b_r, tm):
    t, d = x.shape
    const = lambda i: (0, 0)
    row = pl.BlockSpec((1, d), const)
    return pl.pallas_call(
        _outproj_kernel,
        grid=(t // tm,),
        in_specs=[pl.BlockSpec((tm, d), lambda i: (i, 0)),
                  pl.BlockSpec((tm, d), lambda i: (i, 0)),
                  pl.BlockSpec((d, d), const), row, row, row, row,
                  pl.BlockSpec((d, LANES), const), pl.BlockSpec((1, LANES), const)],
        out_specs=[pl.BlockSpec((tm, d), lambda i: (i, 0)),
                   pl.BlockSpec((tm, d), lambda i: (i, 0)),
                   pl.BlockSpec((tm, LANES), lambda i: (i, 0))],
        out_shape=[jax.ShapeDtypeStruct((t, d), F32), jax.ShapeDtypeStruct((t, d), F32),
                   jax.ShapeDtypeStruct((t, LANES), F32)],
        compiler_params=_cparams(("arbitrary",)),
        name="outproj",
    )(ycat, x, w_out, g1, nw, sc, sh, w_r, b_r)


def _row_gather(src_hbm, idx_ref, dst_ref, sem, n):
    def copy(r):
        return pltpu.make_async_copy(src_hbm.at[pl.ds(idx_ref[r], 1)], dst_ref.at[pl.ds(r, 1)], sem)

    def start(r, c):
        copy(r).start()
        return c

    def wait(r, c):
        copy(r).wait()
        return c

    lax.fori_loop(0, n, start, 0)
    lax.fori_loop(0, n, wait, 0)


def _moe_kernel(bexp_ref, nused_ref, tok_ref, h_hbm, sw_ref, wg_ref, wu_ref, wd_ref, y_ref,
                xbuf, sem):
    b = pl.program_id(0)
    used = b < nused_ref[0]

    @pl.when(used)
    def _():
        rows = xbuf.shape[0]
        _row_gather(h_hbm, tok_ref, xbuf, sem, rows)
        x = xbuf[...].astype(BF16)
        gate = jnp.dot(x, wg_ref[...], preferred_element_type=F32)
        up = jnp.dot(x, wu_ref[...], preferred_element_type=F32)
        hid = (_silu(gate) * up).astype(BF16)
        y = jnp.dot(hid, wd_ref[...], preferred_element_type=F32)
        y_ref[...] = y * sw_ref[...]

    @pl.when(jnp.logical_not(used))
    def _():
        y_ref[...] = jnp.zeros_like(y_ref)


def _moe(h, block_exp, n_used, slot_tok, slot_w, w_gate, w_up, w_down):
    t, d = h.shape
    n_blocks = block_exp.shape[0]
    r = MOE_ROWS
    f = w_gate.shape[2]
    grid_spec = pltpu.PrefetchScalarGridSpec(
        num_scalar_prefetch=2,
        grid=(n_blocks,),
        in_specs=[pl.BlockSpec((None, 1, r), lambda b, be, nu: (b, 0, 0), memory_space=pltpu.SMEM),
                  pl.BlockSpec(memory_space=pl.ANY),
                  pl.BlockSpec((r, 1), lambda b, be, nu: (b, 0)),
                  pl.BlockSpec((None, d, f), lambda b, be, nu: (be[b], 0, 0)),
                  pl.BlockSpec((None, d, f), lambda b, be, nu: (be[b], 0, 0)),
                  pl.BlockSpec((None, f, d), lambda b, be, nu: (be[b], 0, 0))],
        out_specs=pl.BlockSpec((r, d), lambda b, be, nu: (b, 0)),
        scratch_shapes=[pltpu.VMEM((r, d), F32), pltpu.SemaphoreType.DMA(())],
    )
    kern = lambda be, nu, tok, *rest: _moe_kernel(be, nu, tok.at[0], *rest)
    return pl.pallas_call(
        kern,
        grid_spec=grid_spec,
        out_shape=jax.ShapeDtypeStruct((n_blocks * r, d), F32),
        compiler_params=_cparams(("arbitrary",)),
        name="moe",
    )(block_exp, n_used, slot_tok.reshape(n_blocks, 1, r), h, slot_w.reshape(n_blocks * r, 1),
      w_gate, w_up, w_down)


def _final_kernel(pos0_ref, pos1_ref, y_hbm, x1_ref, g2_ref, nw_ref, o_ref, ybuf0, ybuf1, sem0, sem1):
    tm = x1_ref.shape[0]
    p0, p1 = pos0_ref, pos1_ref

    def c0(r):
        return pltpu.make_async_copy(y_hbm.at[pl.ds(p0[r], 1)], ybuf0.at[pl.ds(r, 1)], sem0)

    def c1(r):
        return pltpu.make_async_copy(y_hbm.at[pl.ds(p1[r], 1)], ybuf1.at[pl.ds(r, 1)], sem1)

    def start(r, c):
        c0(r).start()
        c1(r).start()
        return c

    def wait(r, c):
        c0(r).wait()
        c1(r).wait()
        return c

    lax.fori_loop(0, tm, start, 0)
    lax.fori_loop(0, tm, wait, 0)
    x2 = x1_ref[...] + g2_ref[...] * (ybuf0[...] + ybuf1[...])
    ms = jnp.mean(x2 * x2, axis=-1, keepdims=True)
    o_ref[...] = x2 * lax.rsqrt(ms + EPS) * nw_ref[...]


def _final(y_sorted, pos0, pos1, x1, g2, nw, tm):
    t, d = x1.shape
    nt = t // tm
    const = lambda i: (0, 0)
    smem_blk = pl.BlockSpec((None, 1, tm), lambda i: (i, 0, 0), memory_space=pltpu.SMEM)
    kern = lambda a, b_, *rest: _final_kernel(a.at[0], b_.at[0], *rest)
    return pl.pallas_call(
        kern,
        grid=(nt,),
        in_specs=[smem_blk, smem_blk,
                  pl.BlockSpec(memory_space=pl.ANY),
                  pl.BlockSpec((tm, d), lambda i: (i, 0)),
                  pl.BlockSpec((1, d), const), pl.BlockSpec((1, d), const)],
        out_specs=pl.BlockSpec((tm, d), lambda i: (i, 0)),
        out_shape=jax.ShapeDtypeStruct((t, d), F32),
        scratch_shapes=[pltpu.VMEM((tm, d), F32), pltpu.VMEM((tm, d), F32),
                        pltpu.SemaphoreType.DMA(()), pltpu.SemaphoreType.DMA(())],
        compiler_params=_cparams(("arbitrary",)),
        name="final",
    )(pos0.reshape(nt, 1, tm), pos1.reshape(nt, 1, tm), y_sorted, x1, g2, nw)


def _dispatch(route, n_tokens):
    r = MOE_ROWS
    n_assign = 2 * n_tokens
    e_flat = route[:, 0:2].astype(jnp.int32).reshape(n_assign)
    w_flat = route[:, 2:4].reshape(n_assign)
    onehot = (e_flat[:, None] == jnp.arange(N_EXPERTS, dtype=jnp.int32)[None, :]).astype(jnp.int32)
    csum = jnp.cumsum(onehot, axis=0)
    rank = jnp.sum(csum * onehot, axis=1) - 1
    counts = csum[-1]
    padded = ((counts + r - 1) // r) * r
    padded_end = jnp.cumsum(padded)
    pstart = padded_end - padded
    pos = (jnp.sum(pstart[None, :] * onehot, axis=1) + rank).astype(jnp.int32)
    n_blocks = (n_assign + N_EXPERTS * (r - 1) + r - 1) // r
    n_slots = n_blocks * r
    tok_flat = jnp.arange(n_assign, dtype=jnp.int32) // 2
    slot_tok = jnp.zeros((n_slots,), jnp.int32).at[pos].set(tok_flat)
    slot_w = jnp.zeros((n_slots,), F32).at[pos].set(w_flat)
    block_exp = jnp.minimum(
        jnp.searchsorted(padded_end, jnp.arange(n_blocks, dtype=jnp.int32) * r, side='right'),
        N_EXPERTS - 1).astype(jnp.int32)
    n_used = (padded_end[-1] // r).astype(jnp.int32).reshape(1)
    pos2 = pos.reshape(n_tokens, 2)
    return slot_tok, slot_w, block_exp, n_used, pos2[:, 0], pos2[:, 1]


def kernel(x, c, w_ada, b_ada, norm1_w, w_in, conv_w, ssd_conv_w, ssd_conv_b, dt_bias, a_log, d_skip,
           ssd_norm_w, w_out, norm2_w, w_router_grp, b_router_grp, w_router_exp, b_router_exp,
           w_gate, w_up, w_down, final_norm_w):
    bsz, seq, d = x.shape
    assert bsz == 1 and w_ada.shape[0] == 1
    t = bsz * seq
    xt = x.reshape(t, d)
    gw = CONV_GROUP_W
    n_main = 5 * gw + 2 * SSD_GROUPS * SSD_STATE
    n_heads = w_in.shape[2] - n_main
    assert n_heads == SSD_HEADS

    mod = _ada(c, w_ada[0], b_ada[0])
    sh1, sc1, g1, sh2, sc2, g2 = [mod[:, k * d:(k + 1) * d] for k in range(6)]

    w_main = w_in[0, :, :n_main].astype(BF16)
    w_dt = jnp.pad(w_in[0, :, n_main:], ((0, 0), (0, LANES - n_heads))).astype(BF16)
    u, dt_raw = _inproj(xt, norm1_w, sc1, sh1, w_main, w_dt, tm=1024, tn=512)

    pad_h = lambda v: jnp.pad(v, ((0, 0), (0, LANES - n_heads)))
    eexp = (jnp.arange(LANES, dtype=jnp.int32)[:, None]
            == jnp.arange(gw, dtype=jnp.int32)[None, :] // SSD_HEAD_DIM).astype(F32)
    ycat = _mix(u, dt_raw, conv_w[0], ssd_conv_w[0][:, :gw], ssd_conv_w[0][:, gw:],
                ssd_conv_b[:, :gw], ssd_conv_b[:, gw:], pad_h(dt_bias), pad_h(a_log),
                jnp.repeat(d_skip, SSD_HEAD_DIM, axis=1), ssd_norm_w, eexp)

    w_r = jnp.pad(jnp.concatenate([w_router_exp[0], w_router_grp[0]], axis=1),
                  ((0, 0), (0, LANES - N_EXPERTS - N_EXPERT_GROUPS)))
    b_r = jnp.pad(jnp.concatenate([b_router_exp, b_router_grp], axis=1),
                  ((0, 0), (0, LANES - N_EXPERTS - N_EXPERT_GROUPS)))
    x1, h2, route = _outproj(ycat, xt, w_out[0].astype(BF16), g1, norm2_w, sc2, sh2, w_r, b_r, tm=256)

    slot_tok, slot_w, block_exp, n_used, pos0, pos1 = _dispatch(route, t)
    y_sorted = _moe(h2, block_exp, n_used, slot_tok, slot_w,
                    w_gate[0].astype(BF16), w_up[0].astype(BF16), w_down[0].astype(BF16))
    out = _final(y_sorted, pos0, pos1, x1, g2, final_norm_w.reshape(1, d), tm=256)
    return out.reshape(bsz, seq, d)
```

```python
import functools

import jax
import jax.numpy as jnp
from jax import lax
from jax.experimental import pallas as pl
from jax.experimental.pallas import tpu as pltpu

F32 = jnp.float32
BF16 = jnp.bfloat16
U32 = jnp.uint32
I32 = jnp.int32
EPS = 1e-6

CONV_GROUP_W = 1024
SSD_HEADS = 16
SSD_HEAD_DIM = 64
SSD_GROUPS = 2
SSD_STATE = 128
N_EXPERT_GROUPS = 4
EXPERTS_PER_GROUP = 8
N_EXPERTS = 32
LANES = 128
SUBLANES = 8
VMEM_LIMIT = 56 * 1024 * 1024

SSD_CHUNK = 128
MIX_TILE = 256
ROW_TILE = 512
POS_CHUNK = 256
MOE_ROWS = 256
META_LANES = 256
META_EXPERT, META_NVALID, META_NEXT = 0, META_LANES, 2 * META_LANES
HIGHEST = lax.Precision.HIGHEST
NT_DIMS = (((1,), (1,)), ((), ()))


def _silu(v):
    return 0.5 * v * (1.0 + jnp.tanh(0.5 * v))


def _cparams(sem):
    return pltpu.CompilerParams(dimension_semantics=sem, vmem_limit_bytes=VMEM_LIMIT)


def _ada_kernel(c_ref, w_ref, b_ref, o_ref):
    s = _silu(c_ref[...])
    o_ref[...] = jnp.sum(w_ref[...] * s, axis=0, keepdims=True) + b_ref[...]


def _ada(c, w_ada, b_ada):
    d, n = w_ada.shape
    tn = 1024
    return pl.pallas_call(
        _ada_kernel,
        grid=(n // tn,),
        in_specs=[pl.BlockSpec((d, 1), lambda j: (0, 0)),
                  pl.BlockSpec((d, tn), lambda j: (0, j)),
                  pl.BlockSpec((1, tn), lambda j: (0, j))],
        out_specs=pl.BlockSpec((1, tn), lambda j: (0, j)),
        out_shape=jax.ShapeDtypeStruct((1, n), F32),
        compiler_params=_cparams(("arbitrary",)),
        name="ada",
    )(c.reshape(d, 1), w_ada, b_ada.reshape(1, n))


def _inproj_kernel(x_ref, nw_ref, sc_ref, sh_ref, w_ref, wdt_ref, u_ref, dt_ref, h_ref, *, rows, tn):
    tm = x_ref.shape[0]
    n = w_ref.shape[1]
    scale = nw_ref[...] * (1.0 + sc_ref[...])
    for r0 in range(0, tm, rows):
        xf = x_ref[r0:r0 + rows, :]
        ms = jnp.mean(xf * xf, axis=-1, keepdims=True)
        h_ref[r0:r0 + rows, :] = (xf * lax.rsqrt(ms + EPS) * scale + sh_ref[...]).astype(BF16)
        h = h_ref[r0:r0 + rows, :]
        dt_ref[r0:r0 + rows, :] = jnp.dot(h, wdt_ref[...], preferred_element_type=F32)
        for n0 in range(0, n, tn):
            u_ref[r0:r0 + rows, n0:n0 + tn] = jnp.dot(
                h, w_ref[:, n0:n0 + tn], preferred_element_type=F32).astype(BF16)


def _inproj(x, nw, sc, sh, w_main, w_dt):
    t, d = x.shape
    n = w_main.shape[1]
    tm = ROW_TILE
    const = lambda i: (0, 0)
    row = pl.BlockSpec((1, d), const)
    return pl.pallas_call(
        functools.partial(_inproj_kernel, rows=256, tn=512),
        grid=(t // tm,),
        in_specs=[pl.BlockSpec((tm, d), lambda i: (i, 0)), row, row, row,
                  pl.BlockSpec((d, n), const, pipeline_mode=pl.Buffered(1)),
                  pl.BlockSpec((d, LANES), const)],
        out_specs=[pl.BlockSpec((tm, n), lambda i: (i, 0)),
                   pl.BlockSpec((tm, LANES), lambda i: (i, 0))],
        out_shape=[jax.ShapeDtypeStruct((t, n), BF16), jax.ShapeDtypeStruct((t, LANES), F32)],
        scratch_shapes=[pltpu.VMEM((tm, d), BF16)],
        compiler_params=_cparams(("arbitrary",)),
        name="inproj",
    )(x, nw, sc, sh, w_main, w_dt)


def _causal_conv(cur, ext_ref, w_ref, k):
    q = cur.shape[0]
    ext_ref[SUBLANES:SUBLANES + q, :] = cur
    out = w_ref[k - 1:k, :] * cur
    for s in range(1, k):
        out = out + w_ref[k - 1 - s:k - s, :] * ext_ref[SUBLANES - s:SUBLANES - s + q, :]
    ext_ref[0:SUBLANES, :] = cur[q - SUBLANES:, :]
    return out


def _expand_heads(v, left):
    parts = [jnp.where(left, v[:, 2 * k:2 * k + 1], v[:, 2 * k + 1:2 * k + 2])
             for k in range(SSD_HEADS // 2)]
    return jnp.concatenate(parts, axis=1)


def _mix_kernel(ub_ref, uc_ref, uv_ref, uz_ref, uxs_ref, ubc_ref, dt_ref,
                cw_ref, wxs_ref, wbc_ref, bxs_ref, bbc_ref, dtb_ref, alog_ref, dskip_ref,
                nw_ref, o_ref,
                tail_cv, tail_xs, tail_bc, state_ref):
    i = pl.program_id(0)
    tq = ub_ref.shape[0]
    q = SSD_CHUNK
    gw = SSD_HEADS // SSD_GROUPS * SSD_HEAD_DIM
    hpg = SSD_HEADS // SSD_GROUPS

    @pl.when(i == 0)
    def _():
        for ext in (tail_cv, tail_xs, tail_bc):
            ext[0:SUBLANES, :] = jnp.zeros((SUBLANES, ext.shape[1]), F32)
        state_ref[...] = jnp.zeros_like(state_ref)

    cv = uc_ref[...].astype(F32) * uv_ref[...].astype(F32)
    y_conv = ub_ref[...].astype(F32) * _causal_conv(cv, tail_cv, cw_ref, 3)
    o_ref[:, :CONV_GROUP_W] = y_conv.astype(BF16)

    xs_all = _silu(_causal_conv(uxs_ref[...].astype(F32), tail_xs, wxs_ref, 4) + bxs_ref[...])
    bc_all = _silu(_causal_conv(ubc_ref[...].astype(F32), tail_bc, wbc_ref, 4) + bbc_ref[...])
    gs = SSD_GROUPS * SSD_STATE

    dtv = dt_ref[...] + dtb_ref[...]
    dt_all = jnp.maximum(dtv, 0.0) + jnp.log1p(jnp.exp(-jnp.abs(dtv)))
    a_all = dt_all * (-jnp.exp(alog_ref[...]))

    rr = lax.broadcasted_iota(I32, (q, q), 0)
    cc = lax.broadcasted_iota(I32, (q, q), 1)
    causal = rr >= cc
    tri = causal.astype(F32)
    left = lax.broadcasted_iota(I32, (q, LANES), 1) < SSD_HEAD_DIM

    y_chunks = []
    for r0 in range(0, tq, q):
        xs = xs_all[r0:r0 + q, :]
        bm = bc_all[r0:r0 + q, :gs]
        cm = bc_all[r0:r0 + q, gs:]
        dt = dt_all[r0:r0 + q, :]
        a_cs = jnp.dot(tri, a_all[r0:r0 + q, :], precision=HIGHEST,
                       preferred_element_type=F32)
        a_cs_t = a_cs.T
        dt_t = dt.T
        last = a_cs[q - 1:q, :]
        sd_x = _expand_heads(jnp.exp(a_cs), left)
        w_x = _expand_heads(dt * jnp.exp(last - a_cs), left)
        cd_x = _expand_heads(jnp.exp(last), left[0:1, :])
        xs_b = xs.astype(BF16)
        xw_b = (xs * w_x).astype(BF16)

        y_parts = []
        for g in range(SSD_GROUPS):
            cm_g = cm[:, g * SSD_STATE:(g + 1) * SSD_STATE].astype(BF16)
            bm_g = bm[:, g * SSD_STATE:(g + 1) * SSD_STATE]
            cb = lax.dot_general(cm_g, bm_g.astype(BF16), NT_DIMS, preferred_element_type=F32)
            s_prev = state_ref[g]
            y_off = jnp.dot(cm_g, s_prev.astype(BF16), preferred_element_type=F32)
            y_off = y_off * sd_x[:, g * gw:(g + 1) * gw]
            for pr in range(hpg // 2):
                ms = []
                for hh in range(2):
                    h = g * hpg + pr * 2 + hh
                    decay = jnp.exp(a_cs[:, h:h + 1] - a_cs_t[h:h + 1, :])
                    m = jnp.where(causal, decay * cb * dt_t[h:h + 1, :], 0.0)
                    ms.append(m.astype(BF16))
                lhs = jnp.concatenate(ms, axis=1)
                c0 = g * gw + pr * LANES
                xp = xs_b[:, c0:c0 + LANES]
                zero = jnp.zeros_like(xp)
                rhs = jnp.concatenate([jnp.where(left, xp, zero), jnp.where(left, zero, xp)], axis=0)
                y_parts.append(jnp.dot(lhs, rhs, preferred_element_type=F32)
                               + y_off[:, pr * LANES:(pr + 1) * LANES])
            upd = jnp.dot(bm_g.T.astype(BF16), xw_b[:, g * gw:(g + 1) * gw],
                          preferred_element_type=F32)
            state_ref[g] = s_prev * cd_x[:, g * gw:(g + 1) * gw] + upd
        y_chunks.append(jnp.concatenate(y_parts, axis=1))
    y = jnp.concatenate(y_chunks, axis=0) + xs_all * dskip_ref[...]

    yg = y * _silu(uz_ref[...].astype(F32))
    outs = []
    for g in range(SSD_GROUPS):
        v = yg[:, g * gw:(g + 1) * gw]
        ms = jnp.mean(v * v, axis=-1, keepdims=True)
        outs.append(v * lax.rsqrt(ms + EPS))
    y_ssd = jnp.concatenate(outs, axis=1) * nw_ref[...]
    o_ref[:, CONV_GROUP_W:] = y_ssd.astype(BF16)


def _mix(u, dt_raw, conv_w, wxs, wbc, bxs, bbc, dtb, alog, dskip_x, nw):
    t = u.shape[0]
    tq = MIX_TILE
    gw = CONV_GROUP_W
    bcw = 2 * SSD_GROUPS * SSD_STATE
    const = lambda i: (0, 0)
    ublk = lambda k: pl.BlockSpec((tq, gw), lambda i, k=k: (i, k))
    full = lambda arr: pl.BlockSpec(arr.shape, const)
    return pl.pallas_call(
        _mix_kernel,
        grid=(t // tq,),
        in_specs=[ublk(0), ublk(1), ublk(2), ublk(3), ublk(4),
                  pl.BlockSpec((tq, bcw), lambda i: (i, 5 * gw // bcw)),
                  pl.BlockSpec((tq, LANES), lambda i: (i, 0)),
                  full(conv_w), full(wxs), full(wbc), full(bxs), full(bbc), full(dtb), full(alog),
                  full(dskip_x), full(nw)],
        out_specs=pl.BlockSpec((tq, 2 * gw), lambda i: (i, 0)),
        out_shape=jax.ShapeDtypeStruct((t, 2 * gw), BF16),
        scratch_shapes=[pltpu.VMEM((tq + SUBLANES, gw), F32), pltpu.VMEM((tq + SUBLANES, gw), F32),
                        pltpu.VMEM((tq + SUBLANES, bcw), F32),
                        pltpu.VMEM((SSD_GROUPS, SSD_STATE, gw // SSD_GROUPS), F32)],
        compiler_params=_cparams(("arbitrary",)),
        name="mix",
    )(u, u, u, u, u, u, dt_raw, conv_w, wxs, wbc, bxs, bbc, dtb, alog, dskip_x, nw)


def _outproj_kernel(y_ref, x_ref, w_ref, g1_ref, nw_ref, sc_ref, sh_ref, wrh_ref, wrl_ref, br_ref,
                    x1_ref, h_ref, rt_ref, route_ref, *, rows):
    tm, d = x_ref.shape
    scale = nw_ref[...] * (1.0 + sc_ref[...])
    sub = lax.broadcasted_iota(I32, (LANES, rows), 0).astype(F32)
    ninf = jnp.float32(-jnp.inf)
    big = jnp.float32(1e9)
    for c, r0 in enumerate(range(0, tm, rows)):
        acc = jnp.dot(y_ref[r0:r0 + rows, :], w_ref[...], preferred_element_type=F32)
        x1 = x_ref[r0:r0 + rows, :] + g1_ref[...] * acc
        x1_ref[r0:r0 + rows, :] = x1
        ms = jnp.mean(x1 * x1, axis=-1, keepdims=True)
        h = x1 * lax.rsqrt(ms + EPS) * scale + sh_ref[...]
        h_ref[r0:r0 + rows, :] = h
        hb = h.astype(BF16)
        hl = (h - hb.astype(F32)).astype(BF16)

        lt = (lax.dot_general(wrh_ref[...], hb, NT_DIMS, preferred_element_type=F32)
              + lax.dot_general(wrl_ref[...], hb, NT_DIMS, preferred_element_type=F32)
              + lax.dot_general(wrh_ref[...], hl, NT_DIMS, preferred_element_type=F32)
              + br_ref[...])
        is_g = (sub >= N_EXPERTS) & (sub < N_EXPERTS + N_EXPERT_GROUPS)
        gl = jnp.where(is_g, lt, ninf)
        gmax = jnp.max(gl, axis=0, keepdims=True)
        grp_p = 1.0 / jnp.sum(jnp.exp(gl - gmax), axis=0, keepdims=True)
        gidx = jnp.min(jnp.where(gl == gmax, sub - N_EXPERTS, big), axis=0, keepdims=True)
        lo = gidx * EXPERTS_PER_GROUP
        el = jnp.where((sub >= lo) & (sub < lo + EXPERTS_PER_GROUP), lt, ninf)
        m1 = jnp.max(el, axis=0, keepdims=True)
        i1 = jnp.min(jnp.where(el == m1, sub, big), axis=0, keepdims=True)
        el2 = jnp.where(sub == i1, ninf, el)
        m2 = jnp.max(el2, axis=0, keepdims=True)
        i2 = jnp.min(jnp.where(el2 == m2, sub, big), axis=0, keepdims=True)
        e = jnp.exp(m2 - m1)
        w1 = grp_p / (1.0 + e)
        w2 = grp_p * e / (1.0 + e)
        rt = jnp.where(sub == 0, i1, jnp.where(sub == 1, i2,
                       jnp.where(sub == 2, w1, jnp.where(sub == 3, w2, 0.0))))
        rt_ref[c] = rt[:SUBLANES, :]
        route_ref[r0:r0 + rows, :] = rt.T


def _outproj(ycat, x, w_out, g1, nw, sc, sh, wr_hi, wr_lo, b_r):
    t, d = x.shape
    tm = ROW_TILE
    rows = POS_CHUNK
    const = lambda i: (0, 0)
    row = pl.BlockSpec((1, d), const)
    wr = pl.BlockSpec((LANES, d), const)
    return pl.pallas_call(
        functools.partial(_outproj_kernel, rows=rows),
        grid=(t // tm,),
        in_specs=[pl.BlockSpec((tm, d), lambda i: (i, 0)),
                  pl.BlockSpec((tm, d), lambda i: (i, 0)),
                  pl.BlockSpec((d, d), const), row, row, row, row, wr, wr,
                  pl.BlockSpec((LANES, 1), const)],
        out_specs=[pl.BlockSpec((tm, d), lambda i: (i, 0)),
                   pl.BlockSpec((tm, d), lambda i: (i, 0)),
                   pl.BlockSpec((tm // rows, SUBLANES, rows), lambda i: (i, 0, 0)),
                   pl.BlockSpec((tm, LANES), lambda i: (i, 0))],
        out_shape=[jax.ShapeDtypeStruct((t, d), F32), jax.ShapeDtypeStruct((t, d), F32),
                   jax.ShapeDtypeStruct((t // rows, SUBLANES, rows), F32),
                   jax.ShapeDtypeStruct((t, LANES), F32)],
        compiler_params=_cparams(("arbitrary",)),
        name="outproj",
    )(ycat, x, w_out, g1, nw, sc, sh, wr_hi, wr_lo, b_r)


def _meta_kernel(rt_ref, pos_ref, meta_ref):
    nchunk, _, c = rt_ref.shape
    r = float(MOE_ROWS)
    sub = lax.broadcasted_iota(I32, (LANES, c), 0).astype(F32)

    def onehots(k):
        rt = rt_ref[k]
        return (sub == rt[0:1, :]).astype(F32), (sub == rt[1:2, :]).astype(F32)

    def count(k, cnt):
        oh1, oh2 = onehots(k)
        return cnt + jnp.sum(oh1 + oh2, axis=1, keepdims=True)

    cnt = lax.fori_loop(0, nchunk, count, jnp.zeros((LANES, 1), F32))
    padded = jnp.floor((cnt + (r - 1.0)) * (1.0 / r)) * r
    er = lax.broadcasted_iota(I32, (LANES, LANES), 0)
    ec = lax.broadcasted_iota(I32, (LANES, LANES), 1)
    strict_lower = (ec < er).astype(F32)
    pstart = jnp.dot(strict_lower, jnp.broadcast_to(padded, (LANES, LANES)), precision=HIGHEST,
                     preferred_element_type=F32)[:, 0:1]
    tr = lax.broadcasted_iota(I32, (c, c), 0)
    tc = lax.broadcasted_iota(I32, (c, c), 1)
    before = (tr < tc).astype(BF16)

    def place(k, run):
        oh1, oh2 = onehots(k)
        both = oh1 + oh2
        prior = jnp.dot(both.astype(BF16), before, preferred_element_type=F32)
        slot = prior + (pstart + run)
        p1 = jnp.sum(oh1 * slot, axis=0, keepdims=True)
        p2 = jnp.sum(oh2 * slot, axis=0, keepdims=True)
        pos_ref[k] = jnp.concatenate([p1, p2], axis=0).astype(I32)
        return run + jnp.sum(both, axis=1, keepdims=True)

    lax.fori_loop(0, nchunk, place, jnp.zeros((LANES, 1), F32))

    nbl = META_LANES
    e_sub = lax.broadcasted_iota(I32, (LANES, nbl), 0).astype(F32)
    blk0 = lax.broadcasted_iota(I32, (LANES, nbl), 1).astype(F32) * r
    pend = pstart + padded
    bexp = jnp.sum(jnp.where((e_sub < N_EXPERTS) & (pend <= blk0), 1.0, 0.0), axis=0, keepdims=True)
    bexp = jnp.minimum(bexp, N_EXPERTS - 1.0)
    cend = jnp.sum(jnp.where(e_sub == bexp, pstart + cnt, 0.0), axis=0, keepdims=True)
    nvalid = jnp.clip(cend - blk0[0:1, :], 0.0, r)
    total = jnp.sum(padded, axis=0, keepdims=True)
    used = blk0[0:1, :] < total
    nvalid = jnp.where(used, nvalid, 0.0)
    pend_row = jnp.sum(jnp.where(er == ec, pend, 0.0), axis=0, keepdims=True)
    b_sub = lax.broadcasted_iota(I32, (nbl, LANES), 0).astype(F32) * r
    e_lane = lax.broadcasted_iota(I32, (nbl, LANES), 1)
    bexp_col = jnp.sum(jnp.where((e_lane < N_EXPERTS) & (pend_row <= b_sub), 1.0, 0.0),
                       axis=1, keepdims=True)
    bexp_col = jnp.minimum(bexp_col, N_EXPERTS - 1.0)
    seg_end = jnp.sum(jnp.where(e_sub == bexp, pend, 0.0), axis=0, keepdims=True)
    bp_sub = lax.broadcasted_iota(I32, (nbl, nbl), 0).astype(F32) * r
    nxt = jnp.sum(jnp.where(bp_sub == seg_end, bexp_col, 0.0), axis=0, keepdims=True)
    nxt = jnp.where(used & (seg_end < total), nxt, -1.0)
    rowsel = lax.broadcasted_iota(I32, (SUBLANES, nbl), 0)
    meta = jnp.where(rowsel == 0, bexp, jnp.where(rowsel == 1, nvalid, jnp.where(rowsel == 2, nxt, 0.0)))
    meta_ref[...] = meta.astype(I32)


def _meta(rt):
    nchunk, _, c = rt.shape
    return pl.pallas_call(
        _meta_kernel,
        out_shape=[jax.ShapeDtypeStruct((nchunk, 2, c), I32),
                   jax.ShapeDtypeStruct((SUBLANES, META_LANES), I32)],
        compiler_params=pltpu.CompilerParams(vmem_limit_bytes=VMEM_LIMIT),
        name="meta",
    )(rt)


def _dispatch_kernel(meta_ref, pos_ref, h_ref, hs_hbm, zbuf, sem, zsem):
    i = pl.program_id(0)
    tm = h_ref.shape[0]
    nck, _, c = pos_ref.shape
    r = MOE_ROWS

    @pl.when(i == 0)
    def _():
        zbuf[...] = jnp.zeros_like(zbuf)

        def zero_copy(b):
            return pltpu.make_async_copy(zbuf, hs_hbm.at[pl.ds(pl.multiple_of(b * r, r), r)], zsem)

        def zstart(b, carry):
            @pl.when(meta_ref[META_NVALID + b] < r)
            def _():
                zero_copy(b).start()
            return carry

        def zwait(b, carry):
            @pl.when(meta_ref[META_NVALID + b] < r)
            def _():
                zero_copy(b).wait()
            return carry

        lax.fori_loop(0, hs_hbm.shape[0] // r, zstart, 0)
        lax.fori_loop(0, hs_hbm.shape[0] // r, zwait, 0)

    for ck in range(nck):
        for j in range(c):
            row = ck * c + j
            for k in range(2):
                pltpu.make_async_copy(h_ref.at[pl.ds(row, 1)],
                                      hs_hbm.at[pl.ds(pos_ref[ck, k, j], 1)], sem).start()
    for k in range(2):
        pltpu.make_async_copy(h_ref, hs_hbm.at[pl.ds(0, tm)], sem).wait()


def _dispatch(meta_flat, pos, h, n_slots):
    t, d = h.shape
    tm = ROW_TILE
    nck = tm // POS_CHUNK
    grid_spec = pltpu.PrefetchScalarGridSpec(
        num_scalar_prefetch=1,
        grid=(t // tm,),
        in_specs=[pl.BlockSpec((nck, 2, POS_CHUNK), lambda i, m: (i, 0, 0), memory_space=pltpu.SMEM),
                  pl.BlockSpec((tm, d), lambda i, m: (i, 0))],
        out_specs=pl.BlockSpec(memory_space=pl.ANY),
        scratch_shapes=[pltpu.VMEM((MOE_ROWS, d), F32), pltpu.SemaphoreType.DMA(()),
                        pltpu.SemaphoreType.DMA(())],
    )
    return pl.pallas_call(
        _dispatch_kernel,
        grid_spec=grid_spec,
        out_shape=jax.ShapeDtypeStruct((n_slots, d), F32),
        compiler_params=_cparams(("arbitrary",)),
        name="dispatch",
    )(meta_flat, pos, h)


def _expert_weights(meta_ref, w_hbms, stage, sems, bf_refs):
    b = pl.program_id(0)
    e = meta_ref[META_EXPERT + b]
    nxt = meta_ref[META_NEXT + b]
    used = meta_ref[META_NVALID + b] > 0
    prev = meta_ref[META_EXPERT + jnp.maximum(b - 1, 0)]
    changed = jnp.logical_and(used, jnp.logical_or(b == 0, e != prev))

    def copies(ex):
        return [pltpu.make_async_copy(w.at[ex], stage.at[k], sems.at[k]) for k, w in enumerate(w_hbms)]

    @pl.when(b == 0)
    def _():
        for cp in copies(e):
            cp.start()

    @pl.when(changed)
    def _():
        for k, cp in enumerate(copies(e)):
            cp.wait()
            bf_refs[k][...] = stage[k].astype(BF16)

        @pl.when(nxt >= 0)
        def _():
            for cp in copies(nxt):
                cp.start()

    return used


def _moe_up_kernel(meta_ref, x_ref, wg_hbm, wu_hbm, hid_ref, stage, wgb, wub, sems):
    used = _expert_weights(meta_ref, (wg_hbm, wu_hbm), stage, sems, (wgb, wub))

    @pl.when(used)
    def _():
        x = x_ref[...].astype(BF16)
        gate = jnp.dot(x, wgb[...], preferred_element_type=F32)
        up = jnp.dot(x, wub[...], preferred_element_type=F32)
        hid_ref[...] = (_silu(gate) * up).astype(BF16)

    @pl.when(jnp.logical_not(used))
    def _():
        hid_ref[...] = jnp.zeros_like(hid_ref)


def _moe_up(meta_flat, hs, w_gate, w_up):
    n_slots, d = hs.shape
    f = w_gate.shape[2]
    grid_spec = pltpu.PrefetchScalarGridSpec(
        num_scalar_prefetch=1,
        grid=(n_slots // MOE_ROWS,),
        in_specs=[pl.BlockSpec((MOE_ROWS, d), lambda b, m: (b, 0)),
                  pl.BlockSpec(memory_space=pl.ANY), pl.BlockSpec(memory_space=pl.ANY)],
        out_specs=pl.BlockSpec((MOE_ROWS, f), lambda b, m: (b, 0)),
        scratch_shapes=[pltpu.VMEM((2, d, f), F32), pltpu.VMEM((d, f), BF16), pltpu.VMEM((d, f), BF16),
                        pltpu.SemaphoreType.DMA((2,))],
    )
    return pl.pallas_call(
        _moe_up_kernel,
        grid_spec=grid_spec,
        out_shape=jax.ShapeDtypeStruct((n_slots, f), BF16),
        compiler_params=_cparams(("arbitrary",)),
        name="moe_up",
    )(meta_flat, hs, w_gate, w_up)


def _moe_down_kernel(meta_ref, hid_ref, wd_hbm, y_ref, stage, wdb, sems):
    used = _expert_weights(meta_ref, (wd_hbm,), stage, sems, (wdb,))

    @pl.when(used)
    def _():
        y_ref[...] = jnp.dot(hid_ref[...], wdb[...], preferred_element_type=F32)

    @pl.when(jnp.logical_not(used))
    def _():
        y_ref[...] = jnp.zeros_like(y_ref)


def _moe_down(meta_flat, hid, w_down):
    n_slots, f = hid.shape
    d = w_down.shape[2]
    grid_spec = pltpu.PrefetchScalarGridSpec(
        num_scalar_prefetch=1,
        grid=(n_slots // MOE_ROWS,),
        in_specs=[pl.BlockSpec((MOE_ROWS, f), lambda b, m: (b, 0)),
                  pl.BlockSpec(memory_space=pl.ANY)],
        out_specs=pl.BlockSpec((MOE_ROWS, d), lambda b, m: (b, 0)),
        scratch_shapes=[pltpu.VMEM((1, f, d), F32), pltpu.VMEM((f, d), BF16),
                        pltpu.SemaphoreType.DMA((1,))],
    )
    return pl.pallas_call(
        _moe_down_kernel,
        grid_spec=grid_spec,
        out_shape=jax.ShapeDtypeStruct((n_slots, d), F32),
        compiler_params=_cparams(("arbitrary",)),
        name="moe_down",
    )(meta_flat, hid, w_down)


def _final_kernel(pos_ref, posn_ref, y_hbm, x1_ref, route_ref, g2_ref, nw_ref, o_ref, ybuf, sems):
    i = pl.program_id(0)
    nt = pl.num_programs(0)
    tm = x1_ref.shape[0]
    slot = lax.rem(i, 2)

    def issue(p_ref, s):
        for j in range(tm):
            for k in range(2):
                pltpu.make_async_copy(y_hbm.at[pl.ds(p_ref[0, k, j], 1)],
                                      ybuf.at[s, k, pl.ds(j, 1)], sems.at[s]).start()

    @pl.when(i == 0)
    def _():
        issue(pos_ref, 0)

    for s in range(2):
        @pl.when(jnp.logical_and(i + 1 < nt, slot == 1 - s))
        def _(s=s):
            issue(posn_ref, s)

    for k in range(2):
        pltpu.make_async_copy(y_hbm.at[pl.ds(0, tm)], ybuf.at[slot, k], sems.at[slot]).wait()

    rt = route_ref[...]
    moe = rt[:, 2:3] * ybuf[slot, 0] + rt[:, 3:4] * ybuf[slot, 1]
    x2 = x1_ref[...] + g2_ref[...] * moe
    ms = jnp.mean(x2 * x2, axis=-1, keepdims=True)
    o_ref[...] = x2 * lax.rsqrt(ms + EPS) * nw_ref[...]


def _final(y_sorted, pos, x1, route, g2, nw):
    t, d = x1.shape
    tm = POS_CHUNK
    nt = t // tm
    const = lambda i: (0, 0)
    return pl.pallas_call(
        _final_kernel,
        grid=(nt,),
        in_specs=[pl.BlockSpec((1, 2, tm), lambda i: (i, 0, 0), memory_space=pltpu.SMEM),
                  pl.BlockSpec((1, 2, tm), lambda i: (jnp.minimum(i + 1, nt - 1), 0, 0),
                               memory_space=pltpu.SMEM),
                  pl.BlockSpec(memory_space=pl.ANY),
                  pl.BlockSpec((tm, d), lambda i: (i, 0)),
                  pl.BlockSpec((tm, LANES), lambda i: (i, 0)),
                  pl.BlockSpec((1, d), const), pl.BlockSpec((1, d), const)],
        out_specs=pl.BlockSpec((tm, d), lambda i: (i, 0)),
        out_shape=jax.ShapeDtypeStruct((t, d), F32),
        scratch_shapes=[pltpu.VMEM((2, 2, tm, d), F32), pltpu.SemaphoreType.DMA((2,))],
        compiler_params=_cparams(("arbitrary",)),
        name="final",
    )(pos, pos, y_sorted, x1, route, g2, nw)


def kernel(x, c, w_ada, b_ada, norm1_w, w_in, conv_w, ssd_conv_w, ssd_conv_b, dt_bias, a_log, d_skip,
           ssd_norm_w, w_out, norm2_w, w_router_grp, b_router_grp, w_router_exp, b_router_exp,
           w_gate, w_up, w_down, final_norm_w):
    bsz, seq, d = x.shape
    assert bsz == 1 and w_ada.shape[0] == 1
    t = bsz * seq
    assert t % ROW_TILE == 0
    xt = x.reshape(t, d)
    gw = CONV_GROUP_W
    n_main = 5 * gw + 2 * SSD_GROUPS * SSD_STATE
    n_heads = w_in.shape[2] - n_main
    assert n_heads == SSD_HEADS

    mod = _ada(c, w_ada[0], b_ada[0])
    sh1, sc1, g1, sh2, sc2, g2 = [mod[:, k * d:(k + 1) * d] for k in range(6)]

    w_main = w_in[0, :, :n_main].astype(BF16)
    w_dt = jnp.pad(w_in[0, :, n_main:], ((0, 0), (0, LANES - n_heads))).astype(BF16)
    u, dt_raw = _inproj(xt, norm1_w, sc1, sh1, w_main, w_dt)

    pad_h = lambda v: jnp.pad(v, ((0, 0), (0, LANES - n_heads)))
    ycat = _mix(u, dt_raw, conv_w[0], ssd_conv_w[0][:, :gw], ssd_conv_w[0][:, gw:],
                ssd_conv_b[:, :gw], ssd_conv_b[:, gw:], pad_h(dt_bias), pad_h(a_log),
                jnp.repeat(d_skip, SSD_HEAD_DIM, axis=1), ssd_norm_w)

    n_r = N_EXPERTS + N_EXPERT_GROUPS
    w_rt = jnp.pad(jnp.concatenate([w_router_exp[0], w_router_grp[0]], axis=1).T,
                   ((0, LANES - n_r), (0, 0)))
    wr_hi = w_rt.astype(BF16)
    wr_lo = (w_rt - wr_hi.astype(F32)).astype(BF16)
    b_r = jnp.pad(jnp.concatenate([b_router_exp, b_router_grp], axis=1),
                  ((0, 0), (0, LANES - n_r))).reshape(LANES, 1)
    x1, h2, rt, route = _outproj(ycat, xt, w_out[0].astype(BF16), g1, norm2_w, sc2, sh2,
                                 wr_hi, wr_lo, b_r)

    pos, meta = _meta(rt)
    meta_flat = meta.reshape(SUBLANES * META_LANES)
    n_blocks = (2 * t + N_EXPERTS * (MOE_ROWS - 1) + MOE_ROWS - 1) // MOE_ROWS
    assert n_blocks <= META_LANES
    hs = _dispatch(meta_flat, pos, h2, n_blocks * MOE_ROWS)
    hid = _moe_up(meta_flat, hs, w_gate[0], w_up[0])
    y_sorted = _moe_down(meta_flat, hid, w_down[0])
    out = _final(y_sorted, pos, x1, route, g2, final_norm_w.reshape(1, d))
    return out.reshape(bsz, seq, d)
```

```python
import functools

import jax
import jax.numpy as jnp
from jax import lax
from jax.experimental import pallas as pl
from jax.experimental.pallas import tpu as pltpu

F32 = jnp.float32
BF16 = jnp.bfloat16
U32 = jnp.uint32
I32 = jnp.int32
EPS = 1e-6

CONV_GROUP_W = 1024
SSD_HEADS = 16
SSD_HEAD_DIM = 64
SSD_GROUPS = 2
SSD_STATE = 128
N_EXPERT_GROUPS = 4
EXPERTS_PER_GROUP = 8
N_EXPERTS = 32
LANES = 128
SUBLANES = 8
VMEM_LIMIT = 56 * 1024 * 1024

SSD_CHUNK = 128
MIX_TILE = 256
ROW_TILE = 512
POS_CHUNK = 256
MOE_ROWS = 256
META_LANES = 256
META_EXPERT, META_NVALID, META_NEXT = 0, META_LANES, 2 * META_LANES
HIGHEST = lax.Precision.HIGHEST
NT_DIMS = (((1,), (1,)), ((), ()))


def _silu(v):
    return 0.5 * v * (1.0 + jnp.tanh(0.5 * v))


def _cparams(sem):
    return pltpu.CompilerParams(dimension_semantics=sem, vmem_limit_bytes=VMEM_LIMIT)


def _ada_kernel(c_ref, w_ref, b_ref, o_ref):
    s = _silu(c_ref[...])
    o_ref[...] = jnp.sum(w_ref[...] * s, axis=0, keepdims=True) + b_ref[...]


def _ada(c, w_ada, b_ada):
    d, n = w_ada.shape
    tn = 1024
    return pl.pallas_call(
        _ada_kernel,
        grid=(n // tn,),
        in_specs=[pl.BlockSpec((d, 1), lambda j: (0, 0)),
                  pl.BlockSpec((d, tn), lambda j: (0, j)),
                  pl.BlockSpec((1, tn), lambda j: (0, j))],
        out_specs=pl.BlockSpec((1, tn), lambda j: (0, j)),
        out_shape=jax.ShapeDtypeStruct((1, n), F32),
        compiler_params=_cparams(("arbitrary",)),
        name="ada",
    )(c.reshape(d, 1), w_ada, b_ada.reshape(1, n))


def _inproj_kernel(x_ref, nw_ref, sc_ref, sh_ref, w_ref, wdt_ref, u_ref, dt_ref, h_ref, *, rows, tn):
    tm = x_ref.shape[0]
    n = w_ref.shape[1]
    scale = nw_ref[...] * (1.0 + sc_ref[...])
    for r0 in range(0, tm, rows):
        xf = x_ref[r0:r0 + rows, :]
        ms = jnp.mean(xf * xf, axis=-1, keepdims=True)
        h_ref[r0:r0 + rows, :] = (xf * lax.rsqrt(ms + EPS) * scale + sh_ref[...]).astype(BF16)
        h = h_ref[r0:r0 + rows, :]
        dt_ref[r0:r0 + rows, :] = jnp.dot(h, wdt_ref[...], preferred_element_type=F32)
        for n0 in range(0, n, tn):
            u_ref[r0:r0 + rows, n0:n0 + tn] = jnp.dot(
                h, w_ref[:, n0:n0 + tn], preferred_element_type=F32).astype(BF16)


def _inproj(x, nw, sc, sh, w_main, w_dt):
    t, d = x.shape
    n = w_main.shape[1]
    tm = ROW_TILE
    const = lambda i: (0, 0)
    row = pl.BlockSpec((1, d), const)
    return pl.pallas_call(
        functools.partial(_inproj_kernel, rows=256, tn=512),
        grid=(t // tm,),
        in_specs=[pl.BlockSpec((tm, d), lambda i: (i, 0)), row, row, row,
                  pl.BlockSpec((d, n), const, pipeline_mode=pl.Buffered(1)),
                  pl.BlockSpec((d, LANES), const)],
        out_specs=[pl.BlockSpec((tm, n), lambda i: (i, 0)),
                   pl.BlockSpec((tm, LANES), lambda i: (i, 0))],
        out_shape=[jax.ShapeDtypeStruct((t, n), BF16), jax.ShapeDtypeStruct((t, LANES), F32)],
        scratch_shapes=[pltpu.VMEM((tm, d), BF16)],
        compiler_params=_cparams(("arbitrary",)),
        name="inproj",
    )(x, nw, sc, sh, w_main, w_dt)


def _causal_conv(cur, ext_ref, w_ref, k):
    q = cur.shape[0]
    ext_ref[SUBLANES:SUBLANES + q, :] = cur
    out = w_ref[k - 1:k, :] * cur
    for s in range(1, k):
        out = out + w_ref[k - 1 - s:k - s, :] * ext_ref[SUBLANES - s:SUBLANES - s + q, :]
    ext_ref[0:SUBLANES, :] = cur[q - SUBLANES:, :]
    return out


def _expand_heads(v, left):
    parts = [jnp.where(left, v[:, 2 * k:2 * k + 1], v[:, 2 * k + 1:2 * k + 2])
             for k in range(SSD_HEADS // 2)]
    return jnp.concatenate(parts, axis=1)


def _mix_kernel(ub_ref, uc_ref, uv_ref, uz_ref, uxs_ref, ubc_ref, dt_ref,
                cw_ref, wxs_ref, wbc_ref, bxs_ref, bbc_ref, dtb_ref, alog_ref, dskip_ref,
                nw_ref, o_ref,
                tail_cv, tail_xs, tail_bc, state_ref):
    i = pl.program_id(0)
    tq = ub_ref.shape[0]
    q = SSD_CHUNK
    gw = SSD_HEADS // SSD_GROUPS * SSD_HEAD_DIM
    hpg = SSD_HEADS // SSD_GROUPS

    @pl.when(i == 0)
    def _():
        for ext in (tail_cv, tail_xs, tail_bc):
            ext[0:SUBLANES, :] = jnp.zeros((SUBLANES, ext.shape[1]), F32)
        state_ref[...] = jnp.zeros_like(state_ref)

    cv = uc_ref[...].astype(F32) * uv_ref[...].astype(F32)
    y_conv = ub_ref[...].astype(F32) * _causal_conv(cv, tail_cv, cw_ref, 3)
    o_ref[:, :CONV_GROUP_W] = y_conv.astype(BF16)

    xs_all = _silu(_causal_conv(uxs_ref[...].astype(F32), tail_xs, wxs_ref, 4) + bxs_ref[...])
    bc_all = _silu(_causal_conv(ubc_ref[...].astype(F32), tail_bc, wbc_ref, 4) + bbc_ref[...])
    gs = SSD_GROUPS * SSD_STATE

    dtv = dt_ref[...] + dtb_ref[...]
    dt_all = jnp.maximum(dtv, 0.0) + jnp.log1p(jnp.exp(-jnp.abs(dtv)))
    a_all = dt_all * (-jnp.exp(alog_ref[...]))

    rr = lax.broadcasted_iota(I32, (q, q), 0)
    cc = lax.broadcasted_iota(I32, (q, q), 1)
    causal = rr >= cc
    tri = causal.astype(F32)
    left = lax.broadcasted_iota(I32, (q, LANES), 1) < SSD_HEAD_DIM

    y_chunks = []
    for r0 in range(0, tq, q):
        xs = xs_all[r0:r0 + q, :]
        bm = bc_all[r0:r0 + q, :gs]
        cm = bc_all[r0:r0 + q, gs:]
        dt = dt_all[r0:r0 + q, :]
        a_cs = jnp.dot(tri, a_all[r0:r0 + q, :], precision=HIGHEST,
                       preferred_element_type=F32)
        a_cs_t = a_cs.T
        dt_t = dt.T
        last = a_cs[q - 1:q, :]
        sd_x = _expand_heads(jnp.exp(a_cs), left)
        w_x = _expand_heads(dt * jnp.exp(last - a_cs), left)
        cd_x = _expand_heads(jnp.exp(last), left[0:1, :])
        xs_b = xs.astype(BF16)
        xw_b = (xs * w_x).astype(BF16)

        y_parts = []
        for g in range(SSD_GROUPS):
            cm_g = cm[:, g * SSD_STATE:(g + 1) * SSD_STATE].astype(BF16)
            bm_g = bm[:, g * SSD_STATE:(g + 1) * SSD_STATE]
            cb = lax.dot_general(cm_g, bm_g.astype(BF16), NT_DIMS, preferred_element_type=F32)
            s_prev = state_ref[g]
            y_off = jnp.dot(cm_g, s_prev.astype(BF16), preferred_element_type=F32)
            y_off = y_off * sd_x[:, g * gw:(g + 1) * gw]
            for pr in range(hpg // 2):
                ms = []
                for hh in range(2):
                    h = g * hpg + pr * 2 + hh
                    decay = jnp.exp(a_cs[:, h:h + 1] - a_cs_t[h:h + 1, :])
                    m = jnp.where(causal, decay * cb * dt_t[h:h + 1, :], 0.0)
                    ms.append(m.astype(BF16))
                lhs = jnp.concatenate(ms, axis=1)
                c0 = g * gw + pr * LANES
                xp = xs_b[:, c0:c0 + LANES]
                zero = jnp.zeros_like(xp)
                rhs = jnp.concatenate([jnp.where(left, xp, zero), jnp.where(left, zero, xp)], axis=0)
                y_parts.append(jnp.dot(lhs, rhs, preferred_element_type=F32)
                               + y_off[:, pr * LANES:(pr + 1) * LANES])
            upd = jnp.dot(bm_g.T.astype(BF16), xw_b[:, g * gw:(g + 1) * gw],
                          preferred_element_type=F32)
            state_ref[g] = s_prev * cd_x[:, g * gw:(g + 1) * gw] + upd
        y_chunks.append(jnp.concatenate(y_parts, axis=1))
    y = jnp.concatenate(y_chunks, axis=0) + xs_all * dskip_ref[...]

    yg = y * _silu(uz_ref[...].astype(F32))
    outs = []
    for g in range(SSD_GROUPS):
        v = yg[:, g * gw:(g + 1) * gw]
        ms = jnp.mean(v * v, axis=-1, keepdims=True)
        outs.append(v * lax.rsqrt(ms + EPS))
    y_ssd = jnp.concatenate(outs, axis=1) * nw_ref[...]
    o_ref[:, CONV_GROUP_W:] = y_ssd.astype(BF16)


def _mix(u, dt_raw, conv_w, wxs, wbc, bxs, bbc, dtb, alog, dskip_x, nw):
    t = u.shape[0]
    tq = MIX_TILE
    gw = CONV_GROUP_W
    bcw = 2 * SSD_GROUPS * SSD_STATE
    const = lambda i: (0, 0)
    ublk = lambda k: pl.BlockSpec((tq, gw), lambda i, k=k: (i, k))
    full = lambda arr: pl.BlockSpec(arr.shape, const)
    return pl.pallas_call(
        _mix_kernel,
        grid=(t // tq,),
        in_specs=[ublk(0), ublk(1), ublk(2), ublk(3), ublk(4),
                  pl.BlockSpec((tq, bcw), lambda i: (i, 5 * gw // bcw)),
                  pl.BlockSpec((tq, LANES), lambda i: (i, 0)),
                  full(conv_w), full(wxs), full(wbc), full(bxs), full(bbc), full(dtb), full(alog),
                  full(dskip_x), full(nw)],
        out_specs=pl.BlockSpec((tq, 2 * gw), lambda i: (i, 0)),
        out_shape=jax.ShapeDtypeStruct((t, 2 * gw), BF16),
        scratch_shapes=[pltpu.VMEM((tq + SUBLANES, gw), F32), pltpu.VMEM((tq + SUBLANES, gw), F32),
                        pltpu.VMEM((tq + SUBLANES, bcw), F32),
                        pltpu.VMEM((SSD_GROUPS, SSD_STATE, gw // SSD_GROUPS), F32)],
        compiler_params=_cparams(("arbitrary",)),
        name="mix",
    )(u, u, u, u, u, u, dt_raw, conv_w, wxs, wbc, bxs, bbc, dtb, alog, dskip_x, nw)


def _outproj_kernel(y_ref, x_ref, w_ref, g1_ref, nw_ref, sc_ref, sh_ref, wrh_ref, wrl_ref, br_ref,
                    x1_ref, h_ref, rt_ref, route_ref, *, rows):
    tm, d = x_ref.shape
    scale = nw_ref[...] * (1.0 + sc_ref[...])
    sub = lax.broadcasted_iota(I32, (LANES, rows), 0).astype(F32)
    ninf = jnp.float32(-jnp.inf)
    big = jnp.float32(1e9)
    for c, r0 in enumerate(range(0, tm, rows)):
        acc = jnp.dot(y_ref[r0:r0 + rows, :], w_ref[...], preferred_element_type=F32)
        x1 = x_ref[r0:r0 + rows, :] + g1_ref[...] * acc
        x1_ref[r0:r0 + rows, :] = x1
        ms = jnp.mean(x1 * x1, axis=-1, keepdims=True)
        h = x1 * lax.rsqrt(ms + EPS) * scale + sh_ref[...]
        h_ref[r0:r0 + rows, :] = h
        hb = h.astype(BF16)
        hl = (h - hb.astype(F32)).astype(BF16)

        lt = (lax.dot_general(wrh_ref[...], hb, NT_DIMS, preferred_element_type=F32)
              + lax.dot_general(wrl_ref[...], hb, NT_DIMS, preferred_element_type=F32)
              + lax.dot_general(wrh_ref[...], hl, NT_DIMS, preferred_element_type=F32)
              + br_ref[...])
        is_g = (sub >= N_EXPERTS) & (sub < N_EXPERTS + N_EXPERT_GROUPS)
        gl = jnp.where(is_g, lt, ninf)
        gmax = jnp.max(gl, axis=0, keepdims=True)
        grp_p = 1.0 / jnp.sum(jnp.exp(gl - gmax), axis=0, keepdims=True)
        gidx = jnp.min(jnp.where(gl == gmax, sub - N_EXPERTS, big), axis=0, keepdims=True)
        lo = gidx * EXPERTS_PER_GROUP
        el = jnp.where((sub >= lo) & (sub < lo + EXPERTS_PER_GROUP), lt, ninf)
        m1 = jnp.max(el, axis=0, keepdims=True)
        i1 = jnp.min(jnp.where(el == m1, sub, big), axis=0, keepdims=True)
        el2 = jnp.where(sub == i1, ninf, el)
        m2 = jnp.max(el2, axis=0, keepdims=True)
        i2 = jnp.min(jnp.where(el2 == m2, sub, big), axis=0, keepdims=True)
        e = jnp.exp(m2 - m1)
        w1 = grp_p / (1.0 + e)
        w2 = grp_p * e / (1.0 + e)
        rt = jnp.where(sub == 0, i1, jnp.where(sub == 1, i2,
                       jnp.where(sub == 2, w1, jnp.where(sub == 3, w2, 0.0))))
        rt_ref[c] = rt[:SUBLANES, :]
        route_ref[r0:r0 + rows, :] = rt.T


def _outproj(ycat, x, w_out, g1, nw, sc, sh, wr_hi, wr_lo, b_r):
    t, d = x.shape
    tm = ROW_TILE
    rows = POS_CHUNK
    const = lambda i: (0, 0)
    row = pl.BlockSpec((1, d), const)
    wr = pl.BlockSpec((LANES, d), const)
    return pl.pallas_call(
        functools.partial(_outproj_kernel, rows=rows),
        grid=(t // tm,),
        in_specs=[pl.BlockSpec((tm, d), lambda i: (i, 0)),
                  pl.BlockSpec((tm, d), lambda i: (i, 0)),
                  pl.BlockSpec((d, d), const), row, row, row, row, wr, wr,
                  pl.BlockSpec((LANES, 1), const)],
        out_specs=[pl.BlockSpec((tm, d), lambda i: (i, 0)),
                   pl.BlockSpec((tm, d), lambda i: (i, 0)),
                   pl.BlockSpec((tm // rows, SUBLANES, rows), lambda i: (i, 0, 0)),
                   pl.BlockSpec((tm, LANES), lambda i: (i, 0))],
        out_shape=[jax.ShapeDtypeStruct((t, d), F32), jax.ShapeDtypeStruct((t, d), F32),
                   jax.ShapeDtypeStruct((t // rows, SUBLANES, rows), F32),
                   jax.ShapeDtypeStruct((t, LANES), F32)],
        compiler_params=_cparams(("arbitrary",)),
        name="outproj",
    )(ycat, x, w_out, g1, nw, sc, sh, wr_hi, wr_lo, b_r)


def _meta_kernel(rt_ref, pos_ref, meta_ref):
    nchunk, _, c = rt_ref.shape
    r = float(MOE_ROWS)
    sub = lax.broadcasted_iota(I32, (LANES, c), 0).astype(F32)

    def onehots(k):
        rt = rt_ref[k]
        return (sub == rt[0:1, :]).astype(F32), (sub == rt[1:2, :]).astype(F32)

    def count(k, cnt):
        oh1, oh2 = onehots(k)
        return cnt + jnp.sum(oh1 + oh2, axis=1, keepdims=True)

    cnt = lax.fori_loop(0, nchunk, count, jnp.zeros((LANES, 1), F32))
    padded = jnp.floor((cnt + (r - 1.0)) * (1.0 / r)) * r
    er = lax.broadcasted_iota(I32, (LANES, LANES), 0)
    ec = lax.broadcasted_iota(I32, (LANES, LANES), 1)
    strict_lower = (ec < er).astype(F32)
    pstart = jnp.dot(strict_lower, jnp.broadcast_to(padded, (LANES, LANES)), precision=HIGHEST,
                     preferred_element_type=F32)[:, 0:1]
    tr = lax.broadcasted_iota(I32, (c, c), 0)
    tc = lax.broadcasted_iota(I32, (c, c), 1)
    before = (tr < tc).astype(BF16)

    def place(k, run):
        oh1, oh2 = onehots(k)
        both = oh1 + oh2
        prior = jnp.dot(both.astype(BF16), before, preferred_element_type=F32)
        slot = prior + (pstart + run)
        p1 = jnp.sum(oh1 * slot, axis=0, keepdims=True)
        p2 = jnp.sum(oh2 * slot, axis=0, keepdims=True)
        pos_ref[k] = jnp.concatenate([p1, p2], axis=0).astype(I32)
        return run + jnp.sum(both, axis=1, keepdims=True)

    lax.fori_loop(0, nchunk, place, jnp.zeros((LANES, 1), F32))

    nbl = META_LANES
    e_sub = lax.broadcasted_iota(I32, (LANES, nbl), 0).astype(F32)
    blk0 = lax.broadcasted_iota(I32, (LANES, nbl), 1).astype(F32) * r
    pend = pstart + padded
    bexp = jnp.sum(jnp.where((e_sub < N_EXPERTS) & (pend <= blk0), 1.0, 0.0), axis=0, keepdims=True)
    bexp = jnp.minimum(bexp, N_EXPERTS - 1.0)
    cend = jnp.sum(jnp.where(e_sub == bexp, pstart + cnt, 0.0), axis=0, keepdims=True)
    nvalid = jnp.clip(cend - blk0[0:1, :], 0.0, r)
    total = jnp.sum(padded, axis=0, keepdims=True)
    used = blk0[0:1, :] < total
    nvalid = jnp.where(used, nvalid, 0.0)
    pend_row = jnp.sum(jnp.where(er == ec, pend, 0.0), axis=0, keepdims=True)
    b_sub = lax.broadcasted_iota(I32, (nbl, LANES), 0).astype(F32) * r
    e_lane = lax.broadcasted_iota(I32, (nbl, LANES), 1)
    bexp_col = jnp.sum(jnp.where((e_lane < N_EXPERTS) & (pend_row <= b_sub), 1.0, 0.0),
                       axis=1, keepdims=True)
    bexp_col = jnp.minimum(bexp_col, N_EXPERTS - 1.0)
    seg_end = jnp.sum(jnp.where(e_sub == bexp, pend, 0.0), axis=0, keepdims=True)
    bp_sub = lax.broadcasted_iota(I32, (nbl, nbl), 0).astype(F32) * r
    nxt = jnp.sum(jnp.where(bp_sub == seg_end, bexp_col, 0.0), axis=0, keepdims=True)
    nxt = jnp.where(used & (seg_end < total), nxt, -1.0)
    rowsel = lax.broadcasted_iota(I32, (SUBLANES, nbl), 0)
    meta = jnp.where(rowsel == 0, bexp, jnp.where(rowsel == 1, nvalid, jnp.where(rowsel == 2, nxt, 0.0)))
    meta_ref[...] = meta.astype(I32)


def _meta(rt):
    nchunk, _, c = rt.shape
    return pl.pallas_call(
        _meta_kernel,
        out_shape=[jax.ShapeDtypeStruct((nchunk, 2, c), I32),
                   jax.ShapeDtypeStruct((SUBLANES, META_LANES), I32)],
        compiler_params=pltpu.CompilerParams(vmem_limit_bytes=VMEM_LIMIT),
        name="meta",
    )(rt)


def _dispatch_kernel(meta_ref, pos_ref, h_ref, hs_hbm, zbuf, sem, zsem):
    i = pl.program_id(0)
    tm = h_ref.shape[0]
    nck, _, c = pos_ref.shape
    r = MOE_ROWS

    @pl.when(i == 0)
    def _():
        zbuf[...] = jnp.zeros_like(zbuf)

        def zero_copy(b):
            return pltpu.make_async_copy(zbuf, hs_hbm.at[pl.ds(pl.multiple_of(b * r, r), r)], zsem)

        def zstart(b, carry):
            @pl.when(meta_ref[META_NVALID + b] < r)
            def _():
                zero_copy(b).start()
            return carry

        def zwait(b, carry):
            @pl.when(meta_ref[META_NVALID + b] < r)
            def _():
                zero_copy(b).wait()
            return carry

        lax.fori_loop(0, hs_hbm.shape[0] // r, zstart, 0)
        lax.fori_loop(0, hs_hbm.shape[0] // r, zwait, 0)

    for ck in range(nck):
        for j in range(c):
            row = ck * c + j
            for k in range(2):
                pltpu.make_async_copy(h_ref.at[pl.ds(row, 1)],
                                      hs_hbm.at[pl.ds(pos_ref[ck, k, j], 1)], sem).start()
    for k in range(2):
        pltpu.make_async_copy(h_ref, hs_hbm.at[pl.ds(0, tm)], sem).wait()


def _dispatch(meta_flat, pos, h, n_slots):
    t, d = h.shape
    tm = ROW_TILE
    nck = tm // POS_CHUNK
    grid_spec = pltpu.PrefetchScalarGridSpec(
        num_scalar_prefetch=1,
        grid=(t // tm,),
        in_specs=[pl.BlockSpec((nck, 2, POS_CHUNK), lambda i, m: (i, 0, 0), memory_space=pltpu.SMEM),
                  pl.BlockSpec((tm, d), lambda i, m: (i, 0))],
        out_specs=pl.BlockSpec(memory_space=pl.ANY),
        scratch_shapes=[pltpu.VMEM((MOE_ROWS, d), F32), pltpu.SemaphoreType.DMA(()),
                        pltpu.SemaphoreType.DMA(())],
    )
    return pl.pallas_call(
        _dispatch_kernel,
        grid_spec=grid_spec,
        out_shape=jax.ShapeDtypeStruct((n_slots, d), F32),
        compiler_params=_cparams(("arbitrary",)),
        name="dispatch",
    )(meta_flat, pos, h)


def _moe_kernel(meta_ref, x_ref, wg_hbm, wu_hbm, wd_hbm, y_ref, stg_g, stg_u, stg_d, wgb, wub, wdb, sems):
    b = pl.program_id(0)
    e = meta_ref[META_EXPERT + b]
    nxt = meta_ref[META_NEXT + b]
    used = meta_ref[META_NVALID + b] > 0
    prev = meta_ref[META_EXPERT + jnp.maximum(b - 1, 0)]
    changed = jnp.logical_and(used, jnp.logical_or(b == 0, e != prev))
    d, f = wgb.shape
    kc = 512

    def copies(ex):
        return [pltpu.make_async_copy(w.at[ex], stg, sems.at[k])
                for k, (w, stg) in enumerate(((wg_hbm, stg_g), (wu_hbm, stg_u), (wd_hbm, stg_d)))]

    @pl.when(b == 0)
    def _():
        for cp in copies(e):
            cp.start()

    def convert_dot(lhs, stg, wb, n):
        acc = None
        for k0 in range(0, n, kc):
            w = stg[k0:k0 + kc, :].astype(BF16)
            wb[k0:k0 + kc, :] = w
            part = jnp.dot(lhs[:, k0:k0 + kc], w, preferred_element_type=F32)
            acc = part if acc is None else acc + part
        return acc

    @pl.when(changed)
    def _():
        cps = copies(e)
        x = x_ref[...].astype(BF16)
        cps[0].wait()
        gate = convert_dot(x, stg_g, wgb, d)
        cps[1].wait()
        up = convert_dot(x, stg_u, wub, d)
        hid = (_silu(gate) * up).astype(BF16)
        cps[2].wait()
        y_ref[...] = convert_dot(hid, stg_d, wdb, f)

        @pl.when(nxt >= 0)
        def _():
            for cp in copies(nxt):
                cp.start()

    @pl.when(jnp.logical_and(used, jnp.logical_not(changed)))
    def _():
        x = x_ref[...].astype(BF16)
        gate = jnp.dot(x, wgb[...], preferred_element_type=F32)
        up = jnp.dot(x, wub[...], preferred_element_type=F32)
        hid = (_silu(gate) * up).astype(BF16)
        y_ref[...] = jnp.dot(hid, wdb[...], preferred_element_type=F32)

    @pl.when(jnp.logical_not(used))
    def _():
        y_ref[...] = jnp.zeros_like(y_ref)


def _moe(meta_flat, hs, w_gate, w_up, w_down):
    n_slots, d = hs.shape
    f = w_gate.shape[2]
    hbm = pl.BlockSpec(memory_space=pl.ANY)
    grid_spec = pltpu.PrefetchScalarGridSpec(
        num_scalar_prefetch=1,
        grid=(n_slots // MOE_ROWS,),
        in_specs=[pl.BlockSpec((MOE_ROWS, d), lambda b, m: (b, 0)), hbm, hbm, hbm],
        out_specs=pl.BlockSpec((MOE_ROWS, d), lambda b, m: (b, 0)),
        scratch_shapes=[pltpu.VMEM((d, f), F32), pltpu.VMEM((d, f), F32), pltpu.VMEM((f, d), F32),
                        pltpu.VMEM((d, f), BF16), pltpu.VMEM((d, f), BF16), pltpu.VMEM((f, d), BF16),
                        pltpu.SemaphoreType.DMA((3,))],
    )
    return pl.pallas_call(
        _moe_kernel,
        grid_spec=grid_spec,
        out_shape=jax.ShapeDtypeStruct((n_slots, d), F32),
        compiler_params=_cparams(("arbitrary",)),
        name="moe",
    )(meta_flat, hs, w_gate, w_up, w_down)


def _final_kernel(pos_ref, posn_ref, y_hbm, x1_ref, route_ref, g2_ref, nw_ref, o_ref, ybuf, sems):
    i = pl.program_id(0)
    nt = pl.num_programs(0)
    tm = x1_ref.shape[0]
    slot = lax.rem(i, 2)

    def issue(p_ref, s):
        for j in range(tm):
            for k in range(2):
                pltpu.make_async_copy(y_hbm.at[pl.ds(p_ref[0, k, j], 1)],
                                      ybuf.at[s, k, pl.ds(j, 1)], sems.at[s]).start()

    @pl.when(i == 0)
    def _():
        issue(pos_ref, 0)

    for s in range(2):
        @pl.when(jnp.logical_and(i + 1 < nt, slot == 1 - s))
        def _(s=s):
            issue(posn_ref, s)

    for k in range(2):
        pltpu.make_async_copy(y_hbm.at[pl.ds(0, tm)], ybuf.at[slot, k], sems.at[slot]).wait()

    rt = route_ref[...]
    moe = rt[:, 2:3] * ybuf[slot, 0] + rt[:, 3:4] * ybuf[slot, 1]
    x2 = x1_ref[...] + g2_ref[...] * moe
    ms = jnp.mean(x2 * x2, axis=-1, keepdims=True)
    o_ref[...] = x2 * lax.rsqrt(ms + EPS) * nw_ref[...]


def _final(y_sorted, pos, x1, route, g2, nw):
    t, d = x1.shape
    tm = POS_CHUNK
    nt = t // tm
    const = lambda i: (0, 0)
    return pl.pallas_call(
        _final_kernel,
        grid=(nt,),
        in_specs=[pl.BlockSpec((1, 2, tm), lambda i: (i, 0, 0), memory_space=pltpu.SMEM),
                  pl.BlockSpec((1, 2, tm), lambda i: (jnp.minimum(i + 1, nt - 1), 0, 0),
                               memory_space=pltpu.SMEM),
                  pl.BlockSpec(memory_space=pl.ANY),
                  pl.BlockSpec((tm, d), lambda i: (i, 0)),
                  pl.BlockSpec((tm, LANES), lambda i: (i, 0)),
                  pl.BlockSpec((1, d), const), pl.BlockSpec((1, d), const)],
        out_specs=pl.BlockSpec((tm, d), lambda i: (i, 0)),
        out_shape=jax.ShapeDtypeStruct((t, d), F32),
        scratch_shapes=[pltpu.VMEM((2, 2, tm, d), F32), pltpu.SemaphoreType.DMA((2,))],
        compiler_params=_cparams(("arbitrary",)),
        name="final",
    )(pos, pos, y_sorted, x1, route, g2, nw)


def kernel(x, c, w_ada, b_ada, norm1_w, w_in, conv_w, ssd_conv_w, ssd_conv_b, dt_bias, a_log, d_skip,
           ssd_norm_w, w_out, norm2_w, w_router_grp, b_router_grp, w_router_exp, b_router_exp,
           w_gate, w_up, w_down, final_norm_w):
    bsz, seq, d = x.shape
    assert bsz == 1 and w_ada.shape[0] == 1
    t = bsz * seq
    assert t % ROW_TILE == 0
    xt = x.reshape(t, d)
    gw = CONV_GROUP_W
    n_main = 5 * gw + 2 * SSD_GROUPS * SSD_STATE
    n_heads = w_in.shape[2] - n_main
    assert n_heads == SSD_HEADS

    mod = _ada(c, w_ada[0], b_ada[0])
    sh1, sc1, g1, sh2, sc2, g2 = [mod[:, k * d:(k + 1) * d] for k in range(6)]

    w_main = w_in[0, :, :n_main].astype(BF16)
    w_dt = jnp.pad(w_in[0, :, n_main:], ((0, 0), (0, LANES - n_heads))).astype(BF16)
    u, dt_raw = _inproj(xt, norm1_w, sc1, sh1, w_main, w_dt)

    pad_h = lambda v: jnp.pad(v, ((0, 0), (0, LANES - n_heads)))
    ycat = _mix(u, dt_raw, conv_w[0], ssd_conv_w[0][:, :gw], ssd_conv_w[0][:, gw:],
                ssd_conv_b[:, :gw], ssd_conv_b[:, gw:], pad_h(dt_bias), pad_h(a_log),
                jnp.repeat(d_skip, SSD_HEAD_DIM, axis=1), ssd_norm_w)

    n_r = N_EXPERTS + N_EXPERT_GROUPS
    w_rt = jnp.pad(jnp.concatenate([w_router_exp[0], w_router_grp[0]], axis=1).T,
                   ((0, LANES - n_r), (0, 0)))
    wr_hi = w_rt.astype(BF16)
    wr_lo = (w_rt - wr_hi.astype(F32)).astype(BF16)
    b_r = jnp.pad(jnp.concatenate([b_router_exp, b_router_grp], axis=1),
                  ((0, 0), (0, LANES - n_r))).reshape(LANES, 1)
    x1, h2, rt, route = _outproj(ycat, xt, w_out[0].astype(BF16), g1, norm2_w, sc2, sh2,
                                 wr_hi, wr_lo, b_r)

    pos, meta = _meta(rt)
    meta_flat = meta.reshape(SUBLANES * META_LANES)
    n_blocks = (2 * t + N_EXPERTS * (MOE_ROWS - 1) + MOE_ROWS - 1) // MOE_ROWS
    assert n_blocks <= META_LANES
    hs = _dispatch(meta_flat, pos, h2, n_blocks * MOE_ROWS)
    y_sorted = _moe(meta_flat, hs, w_gate[0], w_up[0], w_down[0])
    out = _final(y_sorted, pos, x1, route, g2, final_norm_w.reshape(1, d))
    return out.reshape(bsz, seq, d)
```

```python
import functools

import jax
import jax.numpy as jnp
from jax import lax
from jax.experimental import pallas as pl
from jax.experimental.pallas import tpu as pltpu

F32 = jnp.float32
BF16 = jnp.bfloat16
U32 = jnp.uint32
I32 = jnp.int32
EPS = 1e-6

CONV_GROUP_W = 1024
SSD_HEADS = 16
SSD_HEAD_DIM = 64
SSD_GROUPS = 2
SSD_STATE = 128
N_EXPERT_GROUPS = 4
EXPERTS_PER_GROUP = 8
N_EXPERTS = 32
LANES = 128
SUBLANES = 8
VMEM_LIMIT = 56 * 1024 * 1024

SSD_CHUNK = 128
MIX_TILE = 256
ROW_TILE = 512
POS_CHUNK = 256
MOE_ROWS = 256
META_LANES = 256
META_EXPERT, META_NVALID, META_NEXT = 0, META_LANES, 2 * META_LANES
HIGHEST = lax.Precision.HIGHEST
NT_DIMS = (((1,), (1,)), ((), ()))


def _silu(v):
    return 0.5 * v * (1.0 + jnp.tanh(0.5 * v))


def _cparams(sem):
    return pltpu.CompilerParams(dimension_semantics=sem, vmem_limit_bytes=VMEM_LIMIT)


def _ada_kernel(c_ref, w_ref, b_ref, o_ref):
    s = _silu(c_ref[...])
    o_ref[...] = jnp.sum(w_ref[...] * s, axis=0, keepdims=True) + b_ref[...]


def _ada(c, w_ada, b_ada):
    d, n = w_ada.shape
    tn = 1024
    return pl.pallas_call(
        _ada_kernel,
        grid=(n // tn,),
        in_specs=[pl.BlockSpec((d, 1), lambda j: (0, 0)),
                  pl.BlockSpec((d, tn), lambda j: (0, j)),
                  pl.BlockSpec((1, tn), lambda j: (0, j))],
        out_specs=pl.BlockSpec((1, tn), lambda j: (0, j)),
        out_shape=jax.ShapeDtypeStruct((1, n), F32),
        compiler_params=_cparams(("arbitrary",)),
        name="ada",
    )(c.reshape(d, 1), w_ada, b_ada.reshape(1, n))


def _inproj_kernel(x_ref, nw_ref, sc_ref, sh_ref, w_ref, wdt_ref, u_ref, dt_ref, h_ref, *, rows, tn):
    tm = x_ref.shape[0]
    n = w_ref.shape[1]
    scale = nw_ref[...] * (1.0 + sc_ref[...])
    for r0 in range(0, tm, rows):
        xf = x_ref[r0:r0 + rows, :]
        ms = jnp.mean(xf * xf, axis=-1, keepdims=True)
        h_ref[r0:r0 + rows, :] = (xf * lax.rsqrt(ms + EPS) * scale + sh_ref[...]).astype(BF16)
        h = h_ref[r0:r0 + rows, :]
        dt_ref[r0:r0 + rows, :] = jnp.dot(h, wdt_ref[...], preferred_element_type=F32)
        for n0 in range(0, n, tn):
            u_ref[r0:r0 + rows, n0:n0 + tn] = jnp.dot(
                h, w_ref[:, n0:n0 + tn], preferred_element_type=F32).astype(BF16)


def _inproj(x, nw, sc, sh, w_main, w_dt):
    t, d = x.shape
    n = w_main.shape[1]
    tm = ROW_TILE
    const = lambda i: (0, 0)
    row = pl.BlockSpec((1, d), const)
    return pl.pallas_call(
        functools.partial(_inproj_kernel, rows=256, tn=512),
        grid=(t // tm,),
        in_specs=[pl.BlockSpec((tm, d), lambda i: (i, 0)), row, row, row,
                  pl.BlockSpec((d, n), const, pipeline_mode=pl.Buffered(1)),
                  pl.BlockSpec((d, LANES), const)],
        out_specs=[pl.BlockSpec((tm, n), lambda i: (i, 0)),
                   pl.BlockSpec((tm, LANES), lambda i: (i, 0))],
        out_shape=[jax.ShapeDtypeStruct((t, n), BF16), jax.ShapeDtypeStruct((t, LANES), F32)],
        scratch_shapes=[pltpu.VMEM((tm, d), BF16)],
        compiler_params=_cparams(("arbitrary",)),
        name="inproj",
    )(x, nw, sc, sh, w_main, w_dt)


def _causal_conv(cur, ext_ref, w_ref, k):
    q = cur.shape[0]
    ext_ref[SUBLANES:SUBLANES + q, :] = cur
    out = w_ref[k - 1:k, :] * cur
    for s in range(1, k):
        out = out + w_ref[k - 1 - s:k - s, :] * ext_ref[SUBLANES - s:SUBLANES - s + q, :]
    ext_ref[0:SUBLANES, :] = cur[q - SUBLANES:, :]
    return out


def _expand_heads(v, left):
    parts = [jnp.where(left, v[:, 2 * k:2 * k + 1], v[:, 2 * k + 1:2 * k + 2])
             for k in range(SSD_HEADS // 2)]
    return jnp.concatenate(parts, axis=1)


def _mix_kernel(ub_ref, uc_ref, uv_ref, uz_ref, uxs_ref, ubc_ref, dt_ref,
                cw_ref, wxs_ref, wbc_ref, bxs_ref, bbc_ref, dtb_ref, alog_ref, dskip_ref,
                nw_ref, o_ref,
                tail_cv, tail_xs, tail_bc, state_ref):
    i = pl.program_id(0)
    tq = ub_ref.shape[0]
    q = SSD_CHUNK
    gw = SSD_HEADS // SSD_GROUPS * SSD_HEAD_DIM
    hpg = SSD_HEADS // SSD_GROUPS

    @pl.when(i == 0)
    def _():
        for ext in (tail_cv, tail_xs, tail_bc):
            ext[0:SUBLANES, :] = jnp.zeros((SUBLANES, ext.shape[1]), F32)
        state_ref[...] = jnp.zeros_like(state_ref)

    cv = uc_ref[...].astype(F32) * uv_ref[...].astype(F32)
    y_conv = ub_ref[...].astype(F32) * _causal_conv(cv, tail_cv, cw_ref, 3)
    o_ref[:, :CONV_GROUP_W] = y_conv.astype(BF16)

    xs_all = _silu(_causal_conv(uxs_ref[...].astype(F32), tail_xs, wxs_ref, 4) + bxs_ref[...])
    bc_all = _silu(_causal_conv(ubc_ref[...].astype(F32), tail_bc, wbc_ref, 4) + bbc_ref[...])
    gs = SSD_GROUPS * SSD_STATE

    dtv = dt_ref[...] + dtb_ref[...]
    dt_all = jnp.maximum(dtv, 0.0) + jnp.log1p(jnp.exp(-jnp.abs(dtv)))
    a_all = dt_all * (-jnp.exp(alog_ref[...]))

    rr = lax.broadcasted_iota(I32, (q, q), 0)
    cc = lax.broadcasted_iota(I32, (q, q), 1)
    causal = rr >= cc
    tri = causal.astype(F32)
    left = lax.broadcasted_iota(I32, (q, LANES), 1) < SSD_HEAD_DIM

    y_chunks = []
    for r0 in range(0, tq, q):
        xs = xs_all[r0:r0 + q, :]
        bm = bc_all[r0:r0 + q, :gs]
        cm = bc_all[r0:r0 + q, gs:]
        dt = dt_all[r0:r0 + q, :]
        a_cs = jnp.dot(tri, a_all[r0:r0 + q, :], precision=HIGHEST,
                       preferred_element_type=F32)
        a_cs_t = a_cs.T
        dt_t = dt.T
        last = a_cs[q - 1:q, :]
        sd_x = _expand_heads(jnp.exp(a_cs), left)
        w_x = _expand_heads(dt * jnp.exp(last - a_cs), left)
        cd_x = _expand_heads(jnp.exp(last), left[0:1, :])
        xs_b = xs.astype(BF16)
        xw_b = (xs * w_x).astype(BF16)

        y_parts = []
        for g in range(SSD_GROUPS):
            cm_g = cm[:, g * SSD_STATE:(g + 1) * SSD_STATE].astype(BF16)
            bm_g = bm[:, g * SSD_STATE:(g + 1) * SSD_STATE]
            cb = lax.dot_general(cm_g, bm_g.astype(BF16), NT_DIMS, preferred_element_type=F32)
            s_prev = state_ref[g]
            y_off = jnp.dot(cm_g, s_prev.astype(BF16), preferred_element_type=F32)
            y_off = y_off * sd_x[:, g * gw:(g + 1) * gw]
            for pr in range(hpg // 2):
                ms = []
                for hh in range(2):
                    h = g * hpg + pr * 2 + hh
                    decay = jnp.exp(a_cs[:, h:h + 1] - a_cs_t[h:h + 1, :])
                    m = jnp.where(causal, decay * cb * dt_t[h:h + 1, :], 0.0)
                    ms.append(m.astype(BF16))
                lhs = jnp.concatenate(ms, axis=1)
                c0 = g * gw + pr * LANES
                xp = xs_b[:, c0:c0 + LANES]
                zero = jnp.zeros_like(xp)
                rhs = jnp.concatenate([jnp.where(left, xp, zero), jnp.where(left, zero, xp)], axis=0)
                y_parts.append(jnp.dot(lhs, rhs, preferred_element_type=F32)
                               + y_off[:, pr * LANES:(pr + 1) * LANES])
            upd = jnp.dot(bm_g.T.astype(BF16), xw_b[:, g * gw:(g + 1) * gw],
                          preferred_element_type=F32)
            state_ref[g] = s_prev * cd_x[:, g * gw:(g + 1) * gw] + upd
        y_chunks.append(jnp.concatenate(y_parts, axis=1))
    y = jnp.concatenate(y_chunks, axis=0) + xs_all * dskip_ref[...]

    yg = y * _silu(uz_ref[...].astype(F32))
    outs = []
    for g in range(SSD_GROUPS):
        v = yg[:, g * gw:(g + 1) * gw]
        ms = jnp.mean(v * v, axis=-1, keepdims=True)
        outs.append(v * lax.rsqrt(ms + EPS))
    y_ssd = jnp.concatenate(outs, axis=1) * nw_ref[...]
    o_ref[:, CONV_GROUP_W:] = y_ssd.astype(BF16)


def _mix(u, dt_raw, conv_w, wxs, wbc, bxs, bbc, dtb, alog, dskip_x, nw):
    t = u.shape[0]
    tq = MIX_TILE
    gw = CONV_GROUP_W
    bcw = 2 * SSD_GROUPS * SSD_STATE
    const = lambda i: (0, 0)
    ublk = lambda k: pl.BlockSpec((tq, gw), lambda i, k=k: (i, k))
    full = lambda arr: pl.BlockSpec(arr.shape, const)
    return pl.pallas_call(
        _mix_kernel,
        grid=(t // tq,),
        in_specs=[ublk(0), ublk(1), ublk(2), ublk(3), ublk(4),
                  pl.BlockSpec((tq, bcw), lambda i: (i, 5 * gw // bcw)),
                  pl.BlockSpec((tq, LANES), lambda i: (i, 0)),
                  full(conv_w), full(wxs), full(wbc), full(bxs), full(bbc), full(dtb), full(alog),
                  full(dskip_x), full(nw)],
        out_specs=pl.BlockSpec((tq, 2 * gw), lambda i: (i, 0)),
        out_shape=jax.ShapeDtypeStruct((t, 2 * gw), BF16),
        scratch_shapes=[pltpu.VMEM((tq + SUBLANES, gw), F32), pltpu.VMEM((tq + SUBLANES, gw), F32),
                        pltpu.VMEM((tq + SUBLANES, bcw), F32),
                        pltpu.VMEM((SSD_GROUPS, SSD_STATE, gw // SSD_GROUPS), F32)],
        compiler_params=_cparams(("arbitrary",)),
        name="mix",
    )(u, u, u, u, u, u, dt_raw, conv_w, wxs, wbc, bxs, bbc, dtb, alog, dskip_x, nw)


def _outproj_kernel(y_ref, x_ref, w_ref, g1_ref, nw_ref, sc_ref, sh_ref, wrh_ref, wrl_ref, br_ref,
                    x1_ref, h_ref, rt_ref, route_ref, acc_ref, hb_ref, hl_ref, *, rows):
    tm, d = x_ref.shape
    scale = nw_ref[...] * (1.0 + sc_ref[...])
    for r0 in range(0, tm, rows):
        acc_ref[r0:r0 + rows, :] = jnp.dot(y_ref[r0:r0 + rows, :], w_ref[...],
                                           preferred_element_type=F32)
    for r0 in range(0, tm, rows):
        x1 = x_ref[r0:r0 + rows, :] + g1_ref[...] * acc_ref[r0:r0 + rows, :]
        x1_ref[r0:r0 + rows, :] = x1
        ms = jnp.mean(x1 * x1, axis=-1, keepdims=True)
        h = x1 * lax.rsqrt(ms + EPS) * scale + sh_ref[...]
        h_ref[r0:r0 + rows, :] = h
        hb = h.astype(BF16)
        hb_ref[r0:r0 + rows, :] = hb
        hl_ref[r0:r0 + rows, :] = (h - hb.astype(F32)).astype(BF16)

    hb = hb_ref[...]
    lt = (lax.dot_general(wrh_ref[...], hb, NT_DIMS, preferred_element_type=F32)
          + lax.dot_general(wrl_ref[...], hb, NT_DIMS, preferred_element_type=F32)
          + lax.dot_general(wrh_ref[...], hl_ref[...], NT_DIMS, preferred_element_type=F32)
          + br_ref[...])
    sub = lax.broadcasted_iota(I32, (LANES, tm), 0).astype(F32)
    ninf = jnp.float32(-jnp.inf)
    big = jnp.float32(1e9)
    is_g = (sub >= N_EXPERTS) & (sub < N_EXPERTS + N_EXPERT_GROUPS)
    gl = jnp.where(is_g, lt, ninf)
    gmax = jnp.max(gl, axis=0, keepdims=True)
    grp_p = 1.0 / jnp.sum(jnp.exp(gl - gmax), axis=0, keepdims=True)
    gidx = jnp.min(jnp.where(gl == gmax, sub - N_EXPERTS, big), axis=0, keepdims=True)
    lo = gidx * EXPERTS_PER_GROUP
    el = jnp.where((sub >= lo) & (sub < lo + EXPERTS_PER_GROUP), lt, ninf)
    m1 = jnp.max(el, axis=0, keepdims=True)
    i1 = jnp.min(jnp.where(el == m1, sub, big), axis=0, keepdims=True)
    el2 = jnp.where(sub == i1, ninf, el)
    m2 = jnp.max(el2, axis=0, keepdims=True)
    i2 = jnp.min(jnp.where(el2 == m2, sub, big), axis=0, keepdims=True)
    e = jnp.exp(m2 - m1)
    w1 = grp_p / (1.0 + e)
    w2 = grp_p * e / (1.0 + e)
    rt = jnp.where(sub == 0, i1, jnp.where(sub == 1, i2,
                   jnp.where(sub == 2, w1, jnp.where(sub == 3, w2, 0.0))))
    for c, r0 in enumerate(range(0, tm, rows)):
        rt_ref[c] = rt[:SUBLANES, r0:r0 + rows]
    route_ref[...] = rt.T


def _outproj(ycat, x, w_out, g1, nw, sc, sh, wr_hi, wr_lo, b_r):
    t, d = x.shape
    tm = ROW_TILE
    rows = POS_CHUNK
    const = lambda i: (0, 0)
    row = pl.BlockSpec((1, d), const)
    wr = pl.BlockSpec((LANES, d), const)
    return pl.pallas_call(
        functools.partial(_outproj_kernel, rows=rows),
        grid=(t // tm,),
        in_specs=[pl.BlockSpec((tm, d), lambda i: (i, 0)),
                  pl.BlockSpec((tm, d), lambda i: (i, 0)),
                  pl.BlockSpec((d, d), const), row, row, row, row, wr, wr,
                  pl.BlockSpec((LANES, 1), const)],
        out_specs=[pl.BlockSpec((tm, d), lambda i: (i, 0)),
                   pl.BlockSpec((tm, d), lambda i: (i, 0)),
                   pl.BlockSpec((tm // rows, SUBLANES, rows), lambda i: (i, 0, 0)),
                   pl.BlockSpec((tm, LANES), lambda i: (i, 0))],
        out_shape=[jax.ShapeDtypeStruct((t, d), F32), jax.ShapeDtypeStruct((t, d), F32),
                   jax.ShapeDtypeStruct((t // rows, SUBLANES, rows), F32),
                   jax.ShapeDtypeStruct((t, LANES), F32)],
        scratch_shapes=[pltpu.VMEM((tm, d), F32), pltpu.VMEM((tm, d), BF16), pltpu.VMEM((tm, d), BF16)],
        compiler_params=_cparams(("arbitrary",)),
        name="outproj",
    )(ycat, x, w_out, g1, nw, sc, sh, wr_hi, wr_lo, b_r)


def _meta_kernel(rt_ref, pos_ref, meta_ref):
    nchunk, _, c = rt_ref.shape
    r = float(MOE_ROWS)
    sub = lax.broadcasted_iota(I32, (LANES, c), 0).astype(F32)

    def onehots(k):
        rt = rt_ref[k]
        return (sub == rt[0:1, :]).astype(F32), (sub == rt[1:2, :]).astype(F32)

    def count(k, cnt):
        oh1, oh2 = onehots(k)
        return cnt + jnp.sum(oh1 + oh2, axis=1, keepdims=True)

    cnt = lax.fori_loop(0, nchunk, count, jnp.zeros((LANES, 1), F32))
    padded = jnp.floor((cnt + (r - 1.0)) * (1.0 / r)) * r
    er = lax.broadcasted_iota(I32, (LANES, LANES), 0)
    ec = lax.broadcasted_iota(I32, (LANES, LANES), 1)
    strict_lower = (ec < er).astype(F32)
    pstart = jnp.dot(strict_lower, jnp.broadcast_to(padded, (LANES, LANES)), precision=HIGHEST,
                     preferred_element_type=F32)[:, 0:1]
    tr = lax.broadcasted_iota(I32, (c, c), 0)
    tc = lax.broadcasted_iota(I32, (c, c), 1)
    before = (tr < tc).astype(BF16)

    def place(k, run):
        oh1, oh2 = onehots(k)
        both = oh1 + oh2
        prior = jnp.dot(both.astype(BF16), before, preferred_element_type=F32)
        slot = prior + (pstart + run)
        p1 = jnp.sum(oh1 * slot, axis=0, keepdims=True)
        p2 = jnp.sum(oh2 * slot, axis=0, keepdims=True)
        pos_ref[k] = jnp.concatenate([p1, p2], axis=0).astype(I32)
        return run + jnp.sum(both, axis=1, keepdims=True)

    lax.fori_loop(0, nchunk, place, jnp.zeros((LANES, 1), F32))

    nbl = META_LANES
    e_sub = lax.broadcasted_iota(I32, (LANES, nbl), 0).astype(F32)
    blk0 = lax.broadcasted_iota(I32, (LANES, nbl), 1).astype(F32) * r
    pend = pstart + padded
    bexp = jnp.sum(jnp.where((e_sub < N_EXPERTS) & (pend <= blk0), 1.0, 0.0), axis=0, keepdims=True)
    bexp = jnp.minimum(bexp, N_EXPERTS - 1.0)
    cend = jnp.sum(jnp.where(e_sub == bexp, pstart + cnt, 0.0), axis=0, keepdims=True)
    nvalid = jnp.clip(cend - blk0[0:1, :], 0.0, r)
    total = jnp.sum(padded, axis=0, keepdims=True)
    used = blk0[0:1, :] < total
    nvalid = jnp.where(used, nvalid, 0.0)
    pend_row = jnp.sum(jnp.where(er == ec, pend, 0.0), axis=0, keepdims=True)
    b_sub = lax.broadcasted_iota(I32, (nbl, LANES), 0).astype(F32) * r
    e_lane = lax.broadcasted_iota(I32, (nbl, LANES), 1)
    bexp_col = jnp.sum(jnp.where((e_lane < N_EXPERTS) & (pend_row <= b_sub), 1.0, 0.0),
                       axis=1, keepdims=True)
    bexp_col = jnp.minimum(bexp_col, N_EXPERTS - 1.0)
    seg_end = jnp.sum(jnp.where(e_sub == bexp, pend, 0.0), axis=0, keepdims=True)
    bp_sub = lax.broadcasted_iota(I32, (nbl, nbl), 0).astype(F32) * r
    nxt = jnp.sum(jnp.where(bp_sub == seg_end, bexp_col, 0.0), axis=0, keepdims=True)
    nxt = jnp.where(used & (seg_end < total), nxt, -1.0)
    rowsel = lax.broadcasted_iota(I32, (SUBLANES, nbl), 0)
    meta = jnp.where(rowsel == 0, bexp, jnp.where(rowsel == 1, nvalid, jnp.where(rowsel == 2, nxt, 0.0)))
    meta_ref[...] = meta.astype(I32)


def _meta(rt):
    nchunk, _, c = rt.shape
    return pl.pallas_call(
        _meta_kernel,
        out_shape=[jax.ShapeDtypeStruct((nchunk, 2, c), I32),
                   jax.ShapeDtypeStruct((SUBLANES, META_LANES), I32)],
        compiler_params=pltpu.CompilerParams(vmem_limit_bytes=VMEM_LIMIT),
        name="meta",
    )(rt)


def _dispatch_kernel(meta_ref, pos_ref, h_ref, hs_hbm, zbuf, sem, zsem):
    i = pl.program_id(0)
    tm = h_ref.shape[0]
    nck, _, c = pos_ref.shape
    r = MOE_ROWS

    @pl.when(i == 0)
    def _():
        zbuf[...] = jnp.zeros_like(zbuf)

        def zero_copy(b):
            return pltpu.make_async_copy(zbuf, hs_hbm.at[pl.ds(pl.multiple_of(b * r, r), r)], zsem)

        def zstart(b, carry):
            @pl.when(meta_ref[META_NVALID + b] < r)
            def _():
                zero_copy(b).start()
            return carry

        def zwait(b, carry):
            @pl.when(meta_ref[META_NVALID + b] < r)
            def _():
                zero_copy(b).wait()
            return carry

        lax.fori_loop(0, hs_hbm.shape[0] // r, zstart, 0)
        lax.fori_loop(0, hs_hbm.shape[0] // r, zwait, 0)

    for ck in range(nck):
        for j in range(c):
            row = ck * c + j
            for k in range(2):
                pltpu.make_async_copy(h_ref.at[pl.ds(row, 1)],
                                      hs_hbm.at[pl.ds(pos_ref[ck, k, j], 1)], sem).start(priority=k)
    for k in range(2):
        pltpu.make_async_copy(h_ref, hs_hbm.at[pl.ds(0, tm)], sem).wait()


def _dispatch(meta_flat, pos, h, n_slots):
    t, d = h.shape
    tm = ROW_TILE
    nck = tm // POS_CHUNK
    grid_spec = pltpu.PrefetchScalarGridSpec(
        num_scalar_prefetch=1,
        grid=(t // tm,),
        in_specs=[pl.BlockSpec((nck, 2, POS_CHUNK), lambda i, m: (i, 0, 0), memory_space=pltpu.SMEM),
                  pl.BlockSpec((tm, d), lambda i, m: (i, 0))],
        out_specs=pl.BlockSpec(memory_space=pl.ANY),
        scratch_shapes=[pltpu.VMEM((MOE_ROWS, d), F32), pltpu.SemaphoreType.DMA(()),
                        pltpu.SemaphoreType.DMA(())],
    )
    return pl.pallas_call(
        _dispatch_kernel,
        grid_spec=grid_spec,
        out_shape=jax.ShapeDtypeStruct((n_slots, d), F32),
        compiler_params=_cparams(("arbitrary",)),
        name="dispatch",
    )(meta_flat, pos, h)


def _moe_kernel(meta_ref, x_ref, wg_hbm, wu_hbm, wd_hbm, y_ref, stg_g, stg_u, stg_d, wgb, wub, wdb, sems):
    b = pl.program_id(0)
    e = meta_ref[META_EXPERT + b]
    nxt = meta_ref[META_NEXT + b]
    used = meta_ref[META_NVALID + b] > 0
    prev = meta_ref[META_EXPERT + jnp.maximum(b - 1, 0)]
    changed = jnp.logical_and(used, jnp.logical_or(b == 0, e != prev))
    d, f = wgb.shape
    kc = 512

    def copies(ex):
        return [pltpu.make_async_copy(w.at[ex], stg, sems.at[k])
                for k, (w, stg) in enumerate(((wg_hbm, stg_g), (wu_hbm, stg_u), (wd_hbm, stg_d)))]

    @pl.when(b == 0)
    def _():
        for cp in copies(e):
            cp.start()

    def convert_dot(lhs, stg, wb, n):
        acc = None
        for k0 in range(0, n, kc):
            w = stg[k0:k0 + kc, :].astype(BF16)
            wb[k0:k0 + kc, :] = w
            part = jnp.dot(lhs[:, k0:k0 + kc], w, preferred_element_type=F32)
            acc = part if acc is None else acc + part
        return acc

    @pl.when(changed)
    def _():
        cps = copies(e)
        x = x_ref[...].astype(BF16)
        cps[0].wait()
        gate = convert_dot(x, stg_g, wgb, d)
        cps[1].wait()
        up = convert_dot(x, stg_u, wub, d)
        hid = (_silu(gate) * up).astype(BF16)
        cps[2].wait()
        y_ref[...] = convert_dot(hid, stg_d, wdb, f)

        @pl.when(nxt >= 0)
        def _():
            for cp in copies(nxt):
                cp.start(priority=1)

    @pl.when(jnp.logical_and(used, jnp.logical_not(changed)))
    def _():
        x = x_ref[...].astype(BF16)
        gate = jnp.dot(x, wgb[...], preferred_element_type=F32)
        up = jnp.dot(x, wub[...], preferred_element_type=F32)
        hid = (_silu(gate) * up).astype(BF16)
        y_ref[...] = jnp.dot(hid, wdb[...], preferred_element_type=F32)

    @pl.when(jnp.logical_not(used))
    def _():
        y_ref[...] = jnp.zeros_like(y_ref)


def _moe(meta_flat, hs, w_gate, w_up, w_down):
    n_slots, d = hs.shape
    f = w_gate.shape[2]
    hbm = pl.BlockSpec(memory_space=pl.ANY)
    grid_spec = pltpu.PrefetchScalarGridSpec(
        num_scalar_prefetch=1,
        grid=(n_slots // MOE_ROWS,),
        in_specs=[pl.BlockSpec((MOE_ROWS, d), lambda b, m: (b, 0)), hbm, hbm, hbm],
        out_specs=pl.BlockSpec((MOE_ROWS, d), lambda b, m: (b, 0)),
        scratch_shapes=[pltpu.VMEM((d, f), F32), pltpu.VMEM((d, f), F32), pltpu.VMEM((f, d), F32),
                        pltpu.VMEM((d, f), BF16), pltpu.VMEM((d, f), BF16), pltpu.VMEM((f, d), BF16),
                        pltpu.SemaphoreType.DMA((3,))],
    )
    return pl.pallas_call(
        _moe_kernel,
        grid_spec=grid_spec,
        out_shape=jax.ShapeDtypeStruct((n_slots, d), F32),
        compiler_params=_cparams(("arbitrary",)),
        name="moe",
    )(meta_flat, hs, w_gate, w_up, w_down)


def _final_kernel(pos_ref, posn_ref, y_hbm, x1_ref, route_ref, g2_ref, nw_ref, o_ref, ybuf, sems):
    i = pl.program_id(0)
    nt = pl.num_programs(0)
    tm = x1_ref.shape[0]
    slot = lax.rem(i, 2)

    def issue(p_ref, s):
        for j in range(tm):
            for k in range(2):
                pltpu.make_async_copy(y_hbm.at[pl.ds(p_ref[0, k, j], 1)],
                                      ybuf.at[s, k, pl.ds(j, 1)], sems.at[s]).start(priority=k)

    @pl.when(i == 0)
    def _():
        issue(pos_ref, 0)

    for s in range(2):
        @pl.when(jnp.logical_and(i + 1 < nt, slot == 1 - s))
        def _(s=s):
            issue(posn_ref, s)

    for k in range(2):
        pltpu.make_async_copy(y_hbm.at[pl.ds(0, tm)], ybuf.at[slot, k], sems.at[slot]).wait()

    rt = route_ref[...]
    moe = rt[:, 2:3] * ybuf[slot, 0] + rt[:, 3:4] * ybuf[slot, 1]
    x2 = x1_ref[...] + g2_ref[...] * moe
    ms = jnp.mean(x2 * x2, axis=-1, keepdims=True)
    o_ref[...] = x2 * lax.rsqrt(ms + EPS) * nw_ref[...]


def _final(y_sorted, pos, x1, route, g2, nw):
    t, d = x1.shape
    tm = POS_CHUNK
    nt = t // tm
    const = lambda i: (0, 0)
    return pl.pallas_call(
        _final_kernel,
        grid=(nt,),
        in_specs=[pl.BlockSpec((1, 2, tm), lambda i: (i, 0, 0), memory_space=pltpu.SMEM),
                  pl.BlockSpec((1, 2, tm), lambda i: (jnp.minimum(i + 1, nt - 1), 0, 0),
                               memory_space=pltpu.SMEM),
                  pl.BlockSpec(memory_space=pl.ANY),
                  pl.BlockSpec((tm, d), lambda i: (i, 0)),
                  pl.BlockSpec((tm, LANES), lambda i: (i, 0)),
                  pl.BlockSpec((1, d), const), pl.BlockSpec((1, d), const)],
        out_specs=pl.BlockSpec((tm, d), lambda i: (i, 0)),
        out_shape=jax.ShapeDtypeStruct((t, d), F32),
        scratch_shapes=[pltpu.VMEM((2, 2, tm, d), F32), pltpu.SemaphoreType.DMA((2,))],
        compiler_params=_cparams(("arbitrary",)),
        name="final",
    )(pos, pos, y_sorted, x1, route, g2, nw)


def kernel(x, c, w_ada, b_ada, norm1_w, w_in, conv_w, ssd_conv_w, ssd_conv_b, dt_bias, a_log, d_skip,
           ssd_norm_w, w_out, norm2_w, w_router_grp, b_router_grp, w_router_exp, b_router_exp,
           w_gate, w_up, w_down, final_norm_w):
    bsz, seq, d = x.shape
    assert bsz == 1 and w_ada.shape[0] == 1
    t = bsz * seq
    assert t % ROW_TILE == 0
    xt = x.reshape(t, d)
    gw = CONV_GROUP_W
    n_main = 5 * gw + 2 * SSD_GROUPS * SSD_STATE
    n_heads = w_in.shape[2] - n_main
    assert n_heads == SSD_HEADS

    mod = _ada(c, w_ada[0], b_ada[0])
    sh1, sc1, g1, sh2, sc2, g2 = [mod[:, k * d:(k + 1) * d] for k in range(6)]

    w_main = w_in[0, :, :n_main].astype(BF16)
    w_dt = jnp.pad(w_in[0, :, n_main:], ((0, 0), (0, LANES - n_heads))).astype(BF16)
    u, dt_raw = _inproj(xt, norm1_w, sc1, sh1, w_main, w_dt)

    pad_h = lambda v: jnp.pad(v, ((0, 0), (0, LANES - n_heads)))
    ycat = _mix(u, dt_raw, conv_w[0], ssd_conv_w[0][:, :gw], ssd_conv_w[0][:, gw:],
                ssd_conv_b[:, :gw], ssd_conv_b[:, gw:], pad_h(dt_bias), pad_h(a_log),
                jnp.repeat(d_skip, SSD_HEAD_DIM, axis=1), ssd_norm_w)

    n_r = N_EXPERTS + N_EXPERT_GROUPS
    w_rt = jnp.pad(jnp.concatenate([w_router_exp[0], w_router_grp[0]], axis=1).T,
                   ((0, LANES - n_r), (0, 0)))
    wr_hi = w_rt.astype(BF16)
    wr_lo = (w_rt - wr_hi.astype(F32)).astype(BF16)
    b_r = jnp.pad(jnp.concatenate([b_router_exp, b_router_grp], axis=1),
                  ((0, 0), (0, LANES - n_r))).reshape(LANES, 1)
    x1, h2, rt, route = _outproj(ycat, xt, w_out[0].astype(BF16), g1, norm2_w, sc2, sh2,
                                 wr_hi, wr_lo, b_r)

    pos, meta = _meta(rt)
    meta_flat = meta.reshape(SUBLANES * META_LANES)
    n_blocks = (2 * t + N_EXPERTS * (MOE_ROWS - 1) + MOE_ROWS - 1) // MOE_ROWS
    assert n_blocks <= META_LANES
    hs = _dispatch(meta_flat, pos, h2, n_blocks * MOE_ROWS)
    y_sorted = _moe(meta_flat, hs, w_gate[0], w_up[0], w_down[0])
    out = _final(y_sorted, pos, x1, route, g2, final_norm_w.reshape(1, d))
    return out.reshape(bsz, seq, d)
```

```python
import functools

import jax
import jax.numpy as jnp
from jax import lax
from jax.experimental import pallas as pl
from jax.experimental.pallas import tpu as pltpu

F32 = jnp.float32
BF16 = jnp.bfloat16
U32 = jnp.uint32
I32 = jnp.int32
EPS = 1e-6

CONV_GROUP_W = 1024
SSD_HEADS = 16
SSD_HEAD_DIM = 64
SSD_GROUPS = 2
SSD_STATE = 128
N_EXPERT_GROUPS = 4
EXPERTS_PER_GROUP = 8
N_EXPERTS = 32
LANES = 128
SUBLANES = 8
VMEM_LIMIT = 56 * 1024 * 1024

SSD_CHUNK = 128
MIX_TILE = 256
ROW_TILE = 512
POS_CHUNK = 256
MOE_ROWS = 256
ROUTER_ROWS = 48
META_LANES = 256
META_EXPERT, META_NVALID, META_NEXT = 0, META_LANES, 2 * META_LANES
HIGHEST = lax.Precision.HIGHEST
NT_DIMS = (((1,), (1,)), ((), ()))


def _silu(v):
    return 0.5 * v * (1.0 + jnp.tanh(0.5 * v))


def _cparams(sem):
    return pltpu.CompilerParams(dimension_semantics=sem, vmem_limit_bytes=VMEM_LIMIT)


def _ada_kernel(c_ref, w_ref, b_ref, o_ref):
    s = _silu(c_ref[...])
    o_ref[...] = jnp.sum(w_ref[...] * s, axis=0, keepdims=True) + b_ref[...]


def _ada(c, w_ada, b_ada):
    d, n = w_ada.shape
    tn = 1024
    return pl.pallas_call(
        _ada_kernel,
        grid=(n // tn,),
        in_specs=[pl.BlockSpec((d, 1), lambda j: (0, 0)),
                  pl.BlockSpec((d, tn), lambda j: (0, j)),
                  pl.BlockSpec((1, tn), lambda j: (0, j))],
        out_specs=pl.BlockSpec((1, tn), lambda j: (0, j)),
        out_shape=jax.ShapeDtypeStruct((1, n), F32),
        compiler_params=_cparams(("arbitrary",)),
        name="ada",
    )(c.reshape(d, 1), w_ada, b_ada.reshape(1, n))


def _inproj_kernel(x_ref, nw_ref, sc_ref, sh_ref, w_ref, wdt_ref, u_ref, dt_ref, h_ref, *, rows, tn):
    tm = x_ref.shape[0]
    n = w_ref.shape[1]
    scale = nw_ref[...] * (1.0 + sc_ref[...])
    for r0 in range(0, tm, rows):
        xf = x_ref[r0:r0 + rows, :]
        ms = jnp.mean(xf * xf, axis=-1, keepdims=True)
        h_ref[r0:r0 + rows, :] = (xf * lax.rsqrt(ms + EPS) * scale + sh_ref[...]).astype(BF16)
        h = h_ref[r0:r0 + rows, :]
        dt_ref[r0:r0 + rows, :] = jnp.dot(h, wdt_ref[...], preferred_element_type=F32)
        for n0 in range(0, n, tn):
            u_ref[r0:r0 + rows, n0:n0 + tn] = jnp.dot(
                h, w_ref[:, n0:n0 + tn], preferred_element_type=F32).astype(BF16)


def _inproj(x, nw, sc, sh, w_all, w_dt, n):
    t, d = x.shape
    tm = ROW_TILE
    const = lambda i: (0, 0)
    row = pl.BlockSpec((1, d), const)
    return pl.pallas_call(
        functools.partial(_inproj_kernel, rows=256, tn=512),
        grid=(t // tm,),
        in_specs=[pl.BlockSpec((tm, d), lambda i: (i, 0)), row, row, row,
                  pl.BlockSpec((d, n), const, pipeline_mode=pl.Buffered(1)),
                  pl.BlockSpec((d, LANES), const)],
        out_specs=[pl.BlockSpec((tm, n), lambda i: (i, 0)),
                   pl.BlockSpec((tm, LANES), lambda i: (i, 0))],
        out_shape=[jax.ShapeDtypeStruct((t, n), BF16), jax.ShapeDtypeStruct((t, LANES), F32)],
        scratch_shapes=[pltpu.VMEM((tm, d), BF16)],
        compiler_params=_cparams(("arbitrary",)),
        name="inproj",
    )(x, nw, sc, sh, w_all, w_dt)


def _shift_matrix(q, nshift):
    rr = lax.broadcasted_iota(I32, (q, q), 0)
    cc = lax.broadcasted_iota(I32, (q, q), 1)
    return jnp.concatenate([(rr - cc == s).astype(BF16) for s in range(1, nshift + 1)], axis=0)


def _shifted_rows(smat, x_b, x_f, tail_ref, nshift):
    q = x_b.shape[0]
    sh = jnp.dot(smat[:nshift * q, :], x_b, preferred_element_type=F32)
    head = jnp.concatenate([tail_ref[...], x_f[0:SUBLANES, :]], axis=0)
    outs = [jnp.concatenate([head[SUBLANES - s:2 * SUBLANES - s, :],
                             sh[(s - 1) * q + SUBLANES:s * q, :]], axis=0)
            for s in range(1, nshift + 1)]
    tail_ref[...] = x_f[q - SUBLANES:, :]
    return outs


def _causal_conv4(smat, x_ref, tail_ref, w_ref):
    x_b = x_ref[...]
    x_f = x_b.astype(F32)
    s1, s2, s3 = _shifted_rows(smat, x_b, x_f, tail_ref, 3)
    return w_ref[3:4, :] * x_f + w_ref[2:3, :] * s1 + w_ref[1:2, :] * s2 + w_ref[0:1, :] * s3


def _expand_heads(v, left):
    parts = [jnp.where(left, v[:, 2 * k:2 * k + 1], v[:, 2 * k + 1:2 * k + 2])
             for k in range(SSD_HEADS // 2)]
    return jnp.concatenate(parts, axis=1)


def _mix_kernel(ub_ref, uc_ref, uv_ref, uz_ref, uxs_ref, ubc_ref, dt_ref,
                cw_ref, wxs_ref, wbc_ref, bxs_ref, bbc_ref, dtb_ref, alog_ref, dskip_ref,
                nw_ref, o_ref,
                tail_c, tail_v, tail_xs, tail_bc, state_ref, dec_ref):
    i = pl.program_id(0)
    tq = ub_ref.shape[0]
    q = SSD_CHUNK
    gw = SSD_HEADS // SSD_GROUPS * SSD_HEAD_DIM
    hpg = SSD_HEADS // SSD_GROUPS

    @pl.when(i == 0)
    def _():
        for tail in (tail_c, tail_v, tail_xs, tail_bc):
            tail[...] = jnp.zeros_like(tail)
        state_ref[...] = jnp.zeros_like(state_ref)

    dtv = dt_ref[...] + dtb_ref[...]
    dt_all = jnp.maximum(dtv, 0.0) + jnp.log1p(jnp.exp(-jnp.abs(dtv)))
    a_all = dt_all * (-jnp.exp(alog_ref[...]))
    rr = lax.broadcasted_iota(I32, (q, q), 0)
    cc = lax.broadcasted_iota(I32, (q, q), 1)
    causal = rr >= cc
    tri = causal.astype(F32)
    left = lax.broadcasted_iota(I32, (q, LANES), 1) < SSD_HEAD_DIM
    decays = []
    for ci, r0 in enumerate(range(0, tq, q)):
        dt = dt_all[r0:r0 + q, :]
        a_cs = jnp.dot(tri, a_all[r0:r0 + q, :], precision=HIGHEST,
                       preferred_element_type=F32)
        a_cs_t = a_cs.T
        dt_t = dt.T
        last = a_cs[q - 1:q, :]
        sd_x = _expand_heads(jnp.exp(a_cs), left)
        w_x = _expand_heads(dt * jnp.exp(last - a_cs), left)
        cd_x = _expand_heads(jnp.exp(last), left[0:1, :])
        decays.append((sd_x, w_x, cd_x))
        for h in range(SSD_HEADS):
            decay = jnp.exp(a_cs[:, h:h + 1] - a_cs_t[h:h + 1, :])
            dec_ref[ci * SSD_HEADS + h] = jnp.where(causal, decay * dt_t[h:h + 1, :], 0.0)

    smat = _shift_matrix(tq, 3)

    c_b, v_b = uc_ref[...], uv_ref[...]
    c_f, v_f = c_b.astype(F32), v_b.astype(F32)
    c1, c2 = _shifted_rows(smat, c_b, c_f, tail_c, 2)
    v1, v2 = _shifted_rows(smat, v_b, v_f, tail_v, 2)
    conv = cw_ref[2:3, :] * (c_f * v_f) + cw_ref[1:2, :] * (c1 * v1) + cw_ref[0:1, :] * (c2 * v2)
    o_ref[:, :CONV_GROUP_W] = (ub_ref[...].astype(F32) * conv).astype(BF16)

    xs_all = _silu(_causal_conv4(smat, uxs_ref, tail_xs, wxs_ref) + bxs_ref[...])
    bc_all = _silu(_causal_conv4(smat, ubc_ref, tail_bc, wbc_ref) + bbc_ref[...])
    gs = SSD_GROUPS * SSD_STATE

    y_chunks = []
    for ci, r0 in enumerate(range(0, tq, q)):
        xs = xs_all[r0:r0 + q, :]
        bm = bc_all[r0:r0 + q, :gs]
        cm = bc_all[r0:r0 + q, gs:]
        sd_x, w_x, cd_x = decays[ci]
        xs_b = xs.astype(BF16)
        xw_b = (xs * w_x).astype(BF16)

        y_parts = []
        for g in range(SSD_GROUPS):
            cm_g = cm[:, g * SSD_STATE:(g + 1) * SSD_STATE].astype(BF16)
            bm_g = bm[:, g * SSD_STATE:(g + 1) * SSD_STATE]
            cb = lax.dot_general(cm_g, bm_g.astype(BF16), NT_DIMS, preferred_element_type=F32)
            s_prev = state_ref[g]
            y_off = jnp.dot(cm_g, s_prev.astype(BF16), preferred_element_type=F32)
            y_off = y_off * sd_x[:, g * gw:(g + 1) * gw]
            for pr in range(hpg // 2):
                h0 = ci * SSD_HEADS + g * hpg + pr * 2
                lhs = jnp.concatenate([(cb * dec_ref[h0]).astype(BF16),
                                       (cb * dec_ref[h0 + 1]).astype(BF16)], axis=1)
                c0 = g * gw + pr * LANES
                xp = xs_b[:, c0:c0 + LANES]
                zero = jnp.zeros_like(xp)
                rhs = jnp.concatenate([jnp.where(left, xp, zero), jnp.where(left, zero, xp)], axis=0)
                y_parts.append(jnp.dot(lhs, rhs, preferred_element_type=F32)
                               + y_off[:, pr * LANES:(pr + 1) * LANES])
            upd = jnp.dot(bm_g.T.astype(BF16), xw_b[:, g * gw:(g + 1) * gw],
                          preferred_element_type=F32)
            state_ref[g] = s_prev * cd_x[:, g * gw:(g + 1) * gw] + upd
        y_chunks.append(jnp.concatenate(y_parts, axis=1))
    y = jnp.concatenate(y_chunks, axis=0) + xs_all * dskip_ref[...]

    yg = y * _silu(uz_ref[...].astype(F32))
    outs = []
    for g in range(SSD_GROUPS):
        v = yg[:, g * gw:(g + 1) * gw]
        ms = jnp.mean(v * v, axis=-1, keepdims=True)
        outs.append(v * lax.rsqrt(ms + EPS))
    y_ssd = jnp.concatenate(outs, axis=1) * nw_ref[...]
    o_ref[:, CONV_GROUP_W:] = y_ssd.astype(BF16)


def _mix(u, dt_raw, conv_w, wxs, wbc, bxs, bbc, dtb, alog, dskip_x, nw):
    t = u.shape[0]
    tq = MIX_TILE
    gw = CONV_GROUP_W
    bcw = 2 * SSD_GROUPS * SSD_STATE
    const = lambda i: (0, 0)
    ublk = lambda k: pl.BlockSpec((tq, gw), lambda i, k=k: (i, k))
    full = lambda arr: pl.BlockSpec(arr.shape, const)
    return pl.pallas_call(
        _mix_kernel,
        grid=(t // tq,),
        in_specs=[ublk(0), ublk(1), ublk(2), ublk(3), ublk(4),
                  pl.BlockSpec((tq, bcw), lambda i: (i, 5 * gw // bcw)),
                  pl.BlockSpec((tq, LANES), lambda i: (i, 0)),
                  full(conv_w), full(wxs), full(wbc), full(bxs), full(bbc), full(dtb), full(alog),
                  full(dskip_x), full(nw)],
        out_specs=pl.BlockSpec((tq, 2 * gw), lambda i: (i, 0)),
        out_shape=jax.ShapeDtypeStruct((t, 2 * gw), BF16),
        scratch_shapes=[pltpu.VMEM((SUBLANES, gw), F32), pltpu.VMEM((SUBLANES, gw), F32),
                        pltpu.VMEM((SUBLANES, gw), F32), pltpu.VMEM((SUBLANES, bcw), F32),
                        pltpu.VMEM((SSD_GROUPS, SSD_STATE, gw // SSD_GROUPS), F32),
                        pltpu.VMEM((tq // SSD_CHUNK * SSD_HEADS, SSD_CHUNK, SSD_CHUNK), F32)],
        compiler_params=_cparams(("arbitrary",)),
        name="mix",
    )(u, u, u, u, u, u, dt_raw, conv_w, wxs, wbc, bxs, bbc, dtb, alog, dskip_x, nw)


def _outproj_kernel(y_ref, x_ref, w_ref, g1_ref, nw_ref, sc_ref, sh_ref, wrh_ref, br_ref,
                    x1_ref, h_ref, rt_ref, route_ref, acc_ref, hb_ref, hl_ref, *, rows):
    tm, d = x_ref.shape
    scale = nw_ref[...] * (1.0 + sc_ref[...])
    for r0 in range(0, tm, rows):
        acc_ref[r0:r0 + rows, :] = jnp.dot(y_ref[r0:r0 + rows, :], w_ref[...],
                                           preferred_element_type=F32)
    for r0 in range(0, tm, rows):
        x1 = x_ref[r0:r0 + rows, :] + g1_ref[...] * acc_ref[r0:r0 + rows, :]
        x1_ref[r0:r0 + rows, :] = x1
        ms = jnp.mean(x1 * x1, axis=-1, keepdims=True)
        h = x1 * lax.rsqrt(ms + EPS) * scale + sh_ref[...]
        h_ref[r0:r0 + rows, :] = h
        hb = h.astype(BF16)
        hb_ref[r0:r0 + rows, :] = hb
        hl_ref[r0:r0 + rows, :] = (h - hb.astype(F32)).astype(BF16)

    both = lax.dot_general(wrh_ref[...], hb_ref[...], NT_DIMS, preferred_element_type=F32)
    lt = (both[:ROUTER_ROWS, :] + both[ROUTER_ROWS:, :]
          + lax.dot_general(wrh_ref[:ROUTER_ROWS, :], hl_ref[...], NT_DIMS, preferred_element_type=F32)
          + br_ref[...])
    sub = lax.broadcasted_iota(I32, (ROUTER_ROWS, tm), 0).astype(F32)
    ninf = jnp.float32(-jnp.inf)
    big = jnp.float32(1e9)
    is_g = (sub >= N_EXPERTS) & (sub < N_EXPERTS + N_EXPERT_GROUPS)
    gl = jnp.where(is_g, lt, ninf)
    gmax = jnp.max(gl, axis=0, keepdims=True)
    grp_p = 1.0 / jnp.sum(jnp.exp(gl - gmax), axis=0, keepdims=True)
    gidx = jnp.min(jnp.where(gl == gmax, sub - N_EXPERTS, big), axis=0, keepdims=True)
    lo = gidx * EXPERTS_PER_GROUP
    el = jnp.where((sub >= lo) & (sub < lo + EXPERTS_PER_GROUP), lt, ninf)
    m1 = jnp.max(el, axis=0, keepdims=True)
    i1 = jnp.min(jnp.where(el == m1, sub, big), axis=0, keepdims=True)
    el2 = jnp.where(sub == i1, ninf, el)
    m2 = jnp.max(el2, axis=0, keepdims=True)
    i2 = jnp.min(jnp.where(el2 == m2, sub, big), axis=0, keepdims=True)
    e = jnp.exp(m2 - m1)
    w1 = grp_p / (1.0 + e)
    w2 = grp_p * e / (1.0 + e)
    rt = jnp.where(sub == 0, i1, jnp.where(sub == 1, i2,
                   jnp.where(sub == 2, w1, jnp.where(sub == 3, w2, 0.0))))
    for c, r0 in enumerate(range(0, tm, rows)):
        rt_ref[c] = rt[:SUBLANES, r0:r0 + rows]
    route_ref[...] = jnp.concatenate([rt, jnp.zeros((LANES - ROUTER_ROWS, tm), F32)], axis=0).T


def _outproj(ycat, x, w_out, g1, nw, sc, sh, wr_hilo, b_r):
    t, d = x.shape
    tm = ROW_TILE
    rows = POS_CHUNK
    const = lambda i: (0, 0)
    row = pl.BlockSpec((1, d), const)
    tile = pl.BlockSpec((tm, d), lambda i: (i, 0))
    return pl.pallas_call(
        functools.partial(_outproj_kernel, rows=rows),
        grid=(t // tm,),
        in_specs=[tile, tile,
                  pl.BlockSpec((d, d), const, pipeline_mode=pl.Buffered(1)), row, row, row, row,
                  pl.BlockSpec((2 * ROUTER_ROWS, d), const),
                  pl.BlockSpec((ROUTER_ROWS, 1), const)],
        out_specs=[tile, tile,
                   pl.BlockSpec((tm // rows, SUBLANES, rows), lambda i: (i, 0, 0)),
                   pl.BlockSpec((tm, LANES), lambda i: (i, 0))],
        out_shape=[jax.ShapeDtypeStruct((t, d), F32), jax.ShapeDtypeStruct((t, d), F32),
                   jax.ShapeDtypeStruct((t // rows, SUBLANES, rows), F32),
                   jax.ShapeDtypeStruct((t, LANES), F32)],
        scratch_shapes=[pltpu.VMEM((tm, d), F32), pltpu.VMEM((tm, d), BF16), pltpu.VMEM((tm, d), BF16)],
        compiler_params=_cparams(("arbitrary",)),
        name="outproj",
    )(ycat, x, w_out, g1, nw, sc, sh, wr_hilo, b_r)


def _meta_kernel(rt_ref, pos_ref, meta_ref):
    nchunk, _, c = rt_ref.shape
    r = float(MOE_ROWS)
    sub = lax.broadcasted_iota(I32, (LANES, c), 0).astype(F32)

    def onehots(k):
        rt = rt_ref[k]
        return (sub == rt[0:1, :]).astype(F32), (sub == rt[1:2, :]).astype(F32)

    def count(k, cnt):
        oh1, oh2 = onehots(k)
        return cnt + jnp.sum(oh1 + oh2, axis=1, keepdims=True)

    cnt = lax.fori_loop(0, nchunk, count, jnp.zeros((LANES, 1), F32))
    padded = jnp.floor((cnt + (r - 1.0)) * (1.0 / r)) * r
    er = lax.broadcasted_iota(I32, (LANES, LANES), 0)
    ec = lax.broadcasted_iota(I32, (LANES, LANES), 1)
    strict_lower = (ec < er).astype(F32)
    pstart = jnp.dot(strict_lower, jnp.broadcast_to(padded, (LANES, LANES)), precision=HIGHEST,
                     preferred_element_type=F32)[:, 0:1]
    tr = lax.broadcasted_iota(I32, (c, c), 0)
    tc = lax.broadcasted_iota(I32, (c, c), 1)
    before = (tr < tc).astype(BF16)

    def place(k, run):
        oh1, oh2 = onehots(k)
        both = oh1 + oh2
        prior = jnp.dot(both.astype(BF16), before, preferred_element_type=F32)
        slot = prior + (pstart + run)
        p1 = jnp.sum(oh1 * slot, axis=0, keepdims=True)
        p2 = jnp.sum(oh2 * slot, axis=0, keepdims=True)
        pos_ref[k] = jnp.concatenate([p1, p2], axis=0).astype(I32)
        return run + jnp.sum(both, axis=1, keepdims=True)

    lax.fori_loop(0, nchunk, place, jnp.zeros((LANES, 1), F32))

    nbl = META_LANES
    e_sub = lax.broadcasted_iota(I32, (LANES, nbl), 0).astype(F32)
    blk0 = lax.broadcasted_iota(I32, (LANES, nbl), 1).astype(F32) * r
    pend = pstart + padded
    bexp = jnp.sum(jnp.where((e_sub < N_EXPERTS) & (pend <= blk0), 1.0, 0.0), axis=0, keepdims=True)
    bexp = jnp.minimum(bexp, N_EXPERTS - 1.0)
    cend = jnp.sum(jnp.where(e_sub == bexp, pstart + cnt, 0.0), axis=0, keepdims=True)
    nvalid = jnp.clip(cend - blk0[0:1, :], 0.0, r)
    total = jnp.sum(padded, axis=0, keepdims=True)
    used = blk0[0:1, :] < total
    nvalid = jnp.where(used, nvalid, 0.0)
    pend_row = jnp.sum(jnp.where(er == ec, pend, 0.0), axis=0, keepdims=True)
    b_sub = lax.broadcasted_iota(I32, (nbl, LANES), 0).astype(F32) * r
    e_lane = lax.broadcasted_iota(I32, (nbl, LANES), 1)
    bexp_col = jnp.sum(jnp.where((e_lane < N_EXPERTS) & (pend_row <= b_sub), 1.0, 0.0),
                       axis=1, keepdims=True)
    bexp_col = jnp.minimum(bexp_col, N_EXPERTS - 1.0)
    seg_end = jnp.sum(jnp.where(e_sub == bexp, pend, 0.0), axis=0, keepdims=True)
    bp_sub = lax.broadcasted_iota(I32, (nbl, nbl), 0).astype(F32) * r
    nxt = jnp.sum(jnp.where(bp_sub == seg_end, bexp_col, 0.0), axis=0, keepdims=True)
    nxt = jnp.where(used & (seg_end < total), nxt, -1.0)
    rowsel = lax.broadcasted_iota(I32, (SUBLANES, nbl), 0)
    meta = jnp.where(rowsel == 0, bexp, jnp.where(rowsel == 1, nvalid, jnp.where(rowsel == 2, nxt, 0.0)))
    meta_ref[...] = meta.astype(I32)


def _meta(rt):
    nchunk, _, c = rt.shape
    return pl.pallas_call(
        _meta_kernel,
        out_shape=[jax.ShapeDtypeStruct((nchunk, 2, c), I32),
                   jax.ShapeDtypeStruct((SUBLANES, META_LANES), I32)],
        compiler_params=pltpu.CompilerParams(vmem_limit_bytes=VMEM_LIMIT),
        name="meta",
    )(rt)


def _dispatch_kernel(meta_ref, pos_ref, h_ref, hs_hbm, zbuf, sem, zsem):
    i = pl.program_id(0)
    tm = h_ref.shape[0]
    nck, _, c = pos_ref.shape
    r = MOE_ROWS

    @pl.when(i == 0)
    def _():
        zbuf[...] = jnp.zeros_like(zbuf)

        def zero_copy(b):
            return pltpu.make_async_copy(zbuf, hs_hbm.at[pl.ds(pl.multiple_of(b * r, r), r)], zsem)

        def zstart(b, carry):
            @pl.when(meta_ref[META_NVALID + b] < r)
            def _():
                zero_copy(b).start()
            return carry

        def zwait(b, carry):
            @pl.when(meta_ref[META_NVALID + b] < r)
            def _():
                zero_copy(b).wait()
            return carry

        lax.fori_loop(0, hs_hbm.shape[0] // r, zstart, 0)
        lax.fori_loop(0, hs_hbm.shape[0] // r, zwait, 0)

    for ck in range(nck):
        for j in range(c):
            row = ck * c + j
            for k in range(2):
                pltpu.make_async_copy(h_ref.at[pl.ds(row, 1)],
                                      hs_hbm.at[pl.ds(pos_ref[ck, k, j], 1)], sem).start(priority=k)
    for k in range(2):
        pltpu.make_async_copy(h_ref, hs_hbm.at[pl.ds(0, tm)], sem).wait()


def _dispatch(meta_flat, pos, h, n_slots):
    t, d = h.shape
    tm = ROW_TILE
    nck = tm // POS_CHUNK
    grid_spec = pltpu.PrefetchScalarGridSpec(
        num_scalar_prefetch=1,
        grid=(t // tm,),
        in_specs=[pl.BlockSpec((nck, 2, POS_CHUNK), lambda i, m: (i, 0, 0), memory_space=pltpu.SMEM),
                  pl.BlockSpec((tm, d), lambda i, m: (i, 0))],
        out_specs=pl.BlockSpec(memory_space=pl.ANY),
        scratch_shapes=[pltpu.VMEM((MOE_ROWS, d), F32), pltpu.SemaphoreType.DMA(()),
                        pltpu.SemaphoreType.DMA(())],
    )
    return pl.pallas_call(
        _dispatch_kernel,
        grid_spec=grid_spec,
        out_shape=jax.ShapeDtypeStruct((n_slots, d), F32),
        compiler_params=_cparams(("arbitrary",)),
        name="dispatch",
    )(meta_flat, pos, h)


def _moe_kernel(meta_ref, x_ref, wg_hbm, wu_hbm, wd_hbm, y_ref, stg_g, stg_u, stg_d, wgb, wub, wdb, sems):
    b = pl.program_id(0)
    e = meta_ref[META_EXPERT + b]
    nxt = meta_ref[META_NEXT + b]
    used = meta_ref[META_NVALID + b] > 0
    prev = meta_ref[META_EXPERT + jnp.maximum(b - 1, 0)]
    changed = jnp.logical_and(used, jnp.logical_or(b == 0, e != prev))
    d, f = wgb.shape
    kc = 512

    def copies(ex):
        return [pltpu.make_async_copy(w.at[ex], stg, sems.at[k])
                for k, (w, stg) in enumerate(((wg_hbm, stg_g), (wu_hbm, stg_u), (wd_hbm, stg_d)))]

    @pl.when(b == 0)
    def _():
        for cp in copies(e):
            cp.start()

    def convert_dot(lhs, stg, wb, n):
        acc = None
        for k0 in range(0, n, kc):
            w = stg[k0:k0 + kc, :].astype(BF16)
            wb[k0:k0 + kc, :] = w
            part = jnp.dot(lhs[:, k0:k0 + kc], w, preferred_element_type=F32)
            acc = part if acc is None else acc + part
        return acc

    @pl.when(changed)
    def _():
        cps = copies(e)
        x = x_ref[...].astype(BF16)
        cps[0].wait()
        gate = convert_dot(x, stg_g, wgb, d)
        cps[1].wait()
        up = convert_dot(x, stg_u, wub, d)
        hid = (_silu(gate) * up).astype(BF16)
        cps[2].wait()
        y_ref[...] = convert_dot(hid, stg_d, wdb, f)

        @pl.when(nxt >= 0)
        def _():
            for cp in copies(nxt):
                cp.start(priority=1)

    @pl.when(jnp.logical_and(used, jnp.logical_not(changed)))
    def _():
        x = x_ref[...].astype(BF16)
        gate = jnp.dot(x, wgb[...], preferred_element_type=F32)
        up = jnp.dot(x, wub[...], preferred_element_type=F32)
        hid = (_silu(gate) * up).astype(BF16)
        y_ref[...] = jnp.dot(hid, wdb[...], preferred_element_type=F32)

    @pl.when(jnp.logical_not(used))
    def _():
        y_ref[...] = jnp.zeros_like(y_ref)


def _moe(meta_flat, hs, w_gate, w_up, w_down):
    n_slots, d = hs.shape
    f = w_gate.shape[2]
    hbm = pl.BlockSpec(memory_space=pl.ANY)
    grid_spec = pltpu.PrefetchScalarGridSpec(
        num_scalar_prefetch=1,
        grid=(n_slots // MOE_ROWS,),
        in_specs=[pl.BlockSpec((MOE_ROWS, d), lambda b, m: (b, 0)), hbm, hbm, hbm],
        out_specs=pl.BlockSpec((MOE_ROWS, d), lambda b, m: (b, 0)),
        scratch_shapes=[pltpu.VMEM((d, f), F32), pltpu.VMEM((d, f), F32), pltpu.VMEM((f, d), F32),
                        pltpu.VMEM((d, f), BF16), pltpu.VMEM((d, f), BF16), pltpu.VMEM((f, d), BF16),
                        pltpu.SemaphoreType.DMA((3,))],
    )
    return pl.pallas_call(
        _moe_kernel,
        grid_spec=grid_spec,
        out_shape=jax.ShapeDtypeStruct((n_slots, d), F32),
        compiler_params=_cparams(("arbitrary",)),
        name="moe",
    )(meta_flat, hs, w_gate, w_up, w_down)


def _final_kernel(pos_ref, posn_ref, y_hbm, x1_ref, route_ref, g2_ref, nw_ref, o_ref, ybuf, sems):
    i = pl.program_id(0)
    nt = pl.num_programs(0)
    tm = x1_ref.shape[0]
    slot = lax.rem(i, 2)

    def issue(p_ref, s):
        for j in range(tm):
            for k in range(2):
                pltpu.make_async_copy(y_hbm.at[pl.ds(p_ref[0, k, j], 1)],
                                      ybuf.at[s, k, pl.ds(j, 1)], sems.at[s]).start(priority=k)

    @pl.when(i == 0)
    def _():
        issue(pos_ref, 0)

    for s in range(2):
        @pl.when(jnp.logical_and(i + 1 < nt, slot == 1 - s))
        def _(s=s):
            issue(posn_ref, s)

    for k in range(2):
        pltpu.make_async_copy(y_hbm.at[pl.ds(0, tm)], ybuf.at[slot, k], sems.at[slot]).wait()

    rt = route_ref[...]
    moe = rt[:, 2:3] * ybuf[slot, 0] + rt[:, 3:4] * ybuf[slot, 1]
    x2 = x1_ref[...] + g2_ref[...] * moe
    ms = jnp.mean(x2 * x2, axis=-1, keepdims=True)
    o_ref[...] = x2 * lax.rsqrt(ms + EPS) * nw_ref[...]


def _final(y_sorted, pos, x1, route, g2, nw):
    t, d = x1.shape
    tm = POS_CHUNK
    nt = t // tm
    const = lambda i: (0, 0)
    return pl.pallas_call(
        _final_kernel,
        grid=(nt,),
        in_specs=[pl.BlockSpec((1, 2, tm), lambda i: (i, 0, 0), memory_space=pltpu.SMEM),
                  pl.BlockSpec((1, 2, tm), lambda i: (jnp.minimum(i + 1, nt - 1), 0, 0),
                               memory_space=pltpu.SMEM),
                  pl.BlockSpec(memory_space=pl.ANY),
                  pl.BlockSpec((tm, d), lambda i: (i, 0)),
                  pl.BlockSpec((tm, LANES), lambda i: (i, 0)),
                  pl.BlockSpec((1, d), const), pl.BlockSpec((1, d), const)],
        out_specs=pl.BlockSpec((tm, d), lambda i: (i, 0)),
        out_shape=jax.ShapeDtypeStruct((t, d), F32),
        scratch_shapes=[pltpu.VMEM((2, 2, tm, d), F32), pltpu.SemaphoreType.DMA((2,))],
        compiler_params=_cparams(("arbitrary",)),
        name="final",
    )(pos, pos, y_sorted, x1, route, g2, nw)


def kernel(x, c, w_ada, b_ada, norm1_w, w_in, conv_w, ssd_conv_w, ssd_conv_b, dt_bias, a_log, d_skip,
           ssd_norm_w, w_out, norm2_w, w_router_grp, b_router_grp, w_router_exp, b_router_exp,
           w_gate, w_up, w_down, final_norm_w):
    bsz, seq, d = x.shape
    assert bsz == 1 and w_ada.shape[0] == 1
    t = bsz * seq
    assert t % ROW_TILE == 0
    xt = x.reshape(t, d)
    gw = CONV_GROUP_W
    n_main = 5 * gw + 2 * SSD_GROUPS * SSD_STATE
    n_heads = w_in.shape[2] - n_main
    assert n_heads == SSD_HEADS

    mod = _ada(c, w_ada[0], b_ada[0])
    sh1, sc1, g1, sh2, sc2, g2 = [mod[:, k * d:(k + 1) * d] for k in range(6)]

    w_all = w_in[0].astype(BF16)
    w_dt = jnp.pad(w_all[:, n_main:], ((0, 0), (0, LANES - n_heads)))
    u, dt_raw = _inproj(xt, norm1_w, sc1, sh1, w_all, w_dt, n_main)

    pad_h = lambda v: jnp.pad(v, ((0, 0), (0, LANES - n_heads)))
    ycat = _mix(u, dt_raw, conv_w[0], ssd_conv_w[0][:, :gw], ssd_conv_w[0][:, gw:],
                ssd_conv_b[:, :gw], ssd_conv_b[:, gw:], pad_h(dt_bias), pad_h(a_log),
                jnp.repeat(d_skip, SSD_HEAD_DIM, axis=1), ssd_norm_w)

    n_r = N_EXPERTS + N_EXPERT_GROUPS
    w_rt = jnp.pad(jnp.concatenate([w_router_exp[0], w_router_grp[0]], axis=1).T,
                   ((0, ROUTER_ROWS - n_r), (0, 0)))
    wr_hi = w_rt.astype(BF16)
    wr_lo = (w_rt - wr_hi.astype(F32)).astype(BF16)
    b_r = jnp.pad(jnp.concatenate([b_router_exp, b_router_grp], axis=1),
                  ((0, 0), (0, ROUTER_ROWS - n_r))).reshape(ROUTER_ROWS, 1)
    x1, h2, rt, route = _outproj(ycat, xt, w_out[0].astype(BF16), g1, norm2_w, sc2, sh2,
                                 jnp.concatenate([wr_hi, wr_lo], axis=0), b_r)

    pos, meta = _meta(rt)
    meta_flat = meta.reshape(SUBLANES * META_LANES)
    n_blocks = (2 * t + N_EXPERTS * (MOE_ROWS - 1) + MOE_ROWS - 1) // MOE_ROWS
    assert n_blocks <= META_LANES
    hs = _dispatch(meta_flat, pos, h2, n_blocks * MOE_ROWS)
    y_sorted = _moe(meta_flat, hs, w_gate[0], w_up[0], w_down[0])
    out = _final(y_sorted, pos, x1, route, g2, final_norm_w.reshape(1, d))
    return out.reshape(bsz, seq, d)
```

```python
import functools

import jax
import jax.numpy as jnp
from jax import lax
from jax.experimental import pallas as pl
from jax.experimental.pallas import tpu as pltpu

F32 = jnp.float32
BF16 = jnp.bfloat16
U32 = jnp.uint32
I32 = jnp.int32
EPS = 1e-6

CONV_GROUP_W = 1024
SSD_HEADS = 16
SSD_HEAD_DIM = 64
SSD_GROUPS = 2
SSD_STATE = 128
N_EXPERT_GROUPS = 4
EXPERTS_PER_GROUP = 8
N_EXPERTS = 32
LANES = 128
SUBLANES = 8
VMEM_LIMIT = 56 * 1024 * 1024

SSD_CHUNK = 128
MIX_TILE = 256
CONV_CHUNK = 256
ROW_TILE = 512
POS_CHUNK = 256
MOE_ROWS = 256
ROUTER_ROWS = 48
META_LANES = 256
META_EXPERT, META_NVALID, META_NEXT = 0, META_LANES, 2 * META_LANES
HIGHEST = lax.Precision.HIGHEST
NT_DIMS = (((1,), (1,)), ((), ()))


def _silu(v):
    return 0.5 * v * (1.0 + jnp.tanh(0.5 * v))


def _cparams(sem):
    return pltpu.CompilerParams(dimension_semantics=sem, vmem_limit_bytes=VMEM_LIMIT)


def _ada_kernel(c_ref, w_ref, b_ref, o_ref):
    s = _silu(c_ref[...])
    o_ref[...] = jnp.sum(w_ref[...] * s, axis=0, keepdims=True) + b_ref[...]


def _ada(c, w_ada, b_ada):
    d, n = w_ada.shape
    tn = 1024
    return pl.pallas_call(
        _ada_kernel,
        grid=(n // tn,),
        in_specs=[pl.BlockSpec((d, 1), lambda j: (0, 0)),
                  pl.BlockSpec((d, tn), lambda j: (0, j)),
                  pl.BlockSpec((1, tn), lambda j: (0, j))],
        out_specs=pl.BlockSpec((1, tn), lambda j: (0, j)),
        out_shape=jax.ShapeDtypeStruct((1, n), F32),
        compiler_params=_cparams(("arbitrary",)),
        name="ada",
    )(c.reshape(d, 1), w_ada, b_ada.reshape(1, n))


def _inproj_kernel(x_ref, nw_ref, sc_ref, sh_ref, w_ref, wdt_ref, u_ref, dt_ref, h_ref, *, rows, tn):
    tm = x_ref.shape[0]
    n = w_ref.shape[1]
    scale = nw_ref[...] * (1.0 + sc_ref[...])
    for r0 in range(0, tm, rows):
        xf = x_ref[r0:r0 + rows, :]
        ms = jnp.mean(xf * xf, axis=-1, keepdims=True)
        h_ref[r0:r0 + rows, :] = (xf * lax.rsqrt(ms + EPS) * scale + sh_ref[...]).astype(BF16)
        h = h_ref[r0:r0 + rows, :]
        dt_ref[r0:r0 + rows, :] = jnp.dot(h, wdt_ref[...], preferred_element_type=F32)
        for n0 in range(0, n, tn):
            u_ref[r0:r0 + rows, n0:n0 + tn] = jnp.dot(
                h, w_ref[:, n0:n0 + tn], preferred_element_type=F32).astype(BF16)


def _inproj(x, nw, sc, sh, w_all, w_dt, n):
    t, d = x.shape
    tm = ROW_TILE
    const = lambda i: (0, 0)
    row = pl.BlockSpec((1, d), const)
    return pl.pallas_call(
        functools.partial(_inproj_kernel, rows=256, tn=512),
        grid=(t // tm,),
        in_specs=[pl.BlockSpec((tm, d), lambda i: (i, 0)), row, row, row,
                  pl.BlockSpec((d, n), const, pipeline_mode=pl.Buffered(1)),
                  pl.BlockSpec((d, LANES), const)],
        out_specs=[pl.BlockSpec((tm, n), lambda i: (i, 0)),
                   pl.BlockSpec((tm, LANES), lambda i: (i, 0))],
        out_shape=[jax.ShapeDtypeStruct((t, n), BF16), jax.ShapeDtypeStruct((t, LANES), F32)],
        scratch_shapes=[pltpu.VMEM((tm, d), BF16)],
        compiler_params=_cparams(("arbitrary",)),
        name="inproj",
    )(x, nw, sc, sh, w_all, w_dt)


def _shift_matrix(q, nshift):
    rr = lax.broadcasted_iota(I32, (q, q), 0)
    cc = lax.broadcasted_iota(I32, (q, q), 1)
    return jnp.concatenate([(rr - cc == s).astype(BF16) for s in range(1, nshift + 1)], axis=0)


def _shifted_rows(smat, x_b, x_f, tail_ref, cols, nshift):
    q = x_b.shape[0]
    sh = jnp.dot(smat[:nshift * q, :], x_b, preferred_element_type=F32)
    head = jnp.concatenate([tail_ref[:, cols], x_f[0:SUBLANES, :]], axis=0)
    outs = [jnp.concatenate([head[SUBLANES - s:2 * SUBLANES - s, :],
                             sh[(s - 1) * q + SUBLANES:s * q, :]], axis=0)
            for s in range(1, nshift + 1)]
    tail_ref[:, cols] = x_f[q - SUBLANES:, :]
    return outs


def _conv4_silu(smat, x_ref, tail_ref, w_ref, b_ref, out_ref):
    for c0 in range(0, x_ref.shape[1], CONV_CHUNK):
        cols = slice(c0, c0 + CONV_CHUNK)
        x_b = x_ref[:, cols]
        x_f = x_b.astype(F32)
        s1, s2, s3 = _shifted_rows(smat, x_b, x_f, tail_ref, cols, 3)
        conv = (w_ref[3:4, cols] * x_f + w_ref[2:3, cols] * s1 + w_ref[1:2, cols] * s2
                + w_ref[0:1, cols] * s3)
        out_ref[:, cols] = _silu(conv + b_ref[:, cols])


def _expand_heads(v, left):
    parts = [jnp.where(left, v[:, 2 * k:2 * k + 1], v[:, 2 * k + 1:2 * k + 2])
             for k in range(SSD_HEADS // 2)]
    return jnp.concatenate(parts, axis=1)


def _mix_kernel(ub_ref, uc_ref, uv_ref, uz_ref, uxs_ref, ubc_ref, dt_ref,
                cw_ref, wxs_ref, wbc_ref, bxs_ref, bbc_ref, dtb_ref, alog_ref, dskip_ref,
                nw_ref, o_ref,
                tail_c, tail_v, tail_xs, tail_bc, state_ref, dec_ref, xs_ref, bc_ref):
    i = pl.program_id(0)
    tq = ub_ref.shape[0]
    q = SSD_CHUNK
    gw = SSD_HEADS // SSD_GROUPS * SSD_HEAD_DIM
    hpg = SSD_HEADS // SSD_GROUPS

    @pl.when(i == 0)
    def _():
        for tail in (tail_c, tail_v, tail_xs, tail_bc):
            tail[...] = jnp.zeros_like(tail)
        state_ref[...] = jnp.zeros_like(state_ref)

    dtv = dt_ref[...] + dtb_ref[...]
    dt_all = jnp.maximum(dtv, 0.0) + jnp.log1p(jnp.exp(-jnp.abs(dtv)))
    a_all = dt_all * (-jnp.exp(alog_ref[...]))
    rr = lax.broadcasted_iota(I32, (q, q), 0)
    cc = lax.broadcasted_iota(I32, (q, q), 1)
    causal = rr >= cc
    tri = causal.astype(F32)
    left = lax.broadcasted_iota(I32, (q, LANES), 1) < SSD_HEAD_DIM
    decays = []
    for ci, r0 in enumerate(range(0, tq, q)):
        dt = dt_all[r0:r0 + q, :]
        a_cs = jnp.dot(tri, a_all[r0:r0 + q, :], precision=HIGHEST,
                       preferred_element_type=F32)
        a_cs_t = a_cs.T
        dt_t = dt.T
        last = a_cs[q - 1:q, :]
        sd_x = _expand_heads(jnp.exp(a_cs), left)
        w_x = _expand_heads(dt * jnp.exp(last - a_cs), left)
        cd_x = _expand_heads(jnp.exp(last), left[0:1, :])
        decays.append((sd_x, w_x, cd_x))
        for h in range(SSD_HEADS):
            decay = jnp.exp(a_cs[:, h:h + 1] - a_cs_t[h:h + 1, :])
            dec_ref[ci * SSD_HEADS + h] = jnp.where(causal, decay * dt_t[h:h + 1, :], 0.0)

    smat = _shift_matrix(tq, 3)

    for c0 in range(0, CONV_GROUP_W, CONV_CHUNK):
        cols = slice(c0, c0 + CONV_CHUNK)
        c_b, v_b = uc_ref[:, cols], uv_ref[:, cols]
        c_f, v_f = c_b.astype(F32), v_b.astype(F32)
        c1, c2 = _shifted_rows(smat, c_b, c_f, tail_c, cols, 2)
        v1, v2 = _shifted_rows(smat, v_b, v_f, tail_v, cols, 2)
        conv = (cw_ref[2:3, cols] * (c_f * v_f) + cw_ref[1:2, cols] * (c1 * v1)
                + cw_ref[0:1, cols] * (c2 * v2))
        o_ref[:, cols] = (ub_ref[:, cols].astype(F32) * conv).astype(BF16)

    _conv4_silu(smat, uxs_ref, tail_xs, wxs_ref, bxs_ref, xs_ref)
    _conv4_silu(smat, ubc_ref, tail_bc, wbc_ref, bbc_ref, bc_ref)
    xs_all = xs_ref[...]
    bc_all = bc_ref[...]
    gs = SSD_GROUPS * SSD_STATE

    y_chunks = []
    for ci, r0 in enumerate(range(0, tq, q)):
        xs = xs_all[r0:r0 + q, :]
        bm = bc_all[r0:r0 + q, :gs]
        cm = bc_all[r0:r0 + q, gs:]
        sd_x, w_x, cd_x = decays[ci]
        xs_b = xs.astype(BF16)
        xw_b = (xs * w_x).astype(BF16)

        y_parts = []
        for g in range(SSD_GROUPS):
            cm_g = cm[:, g * SSD_STATE:(g + 1) * SSD_STATE].astype(BF16)
            bm_g = bm[:, g * SSD_STATE:(g + 1) * SSD_STATE]
            cb = lax.dot_general(cm_g, bm_g.astype(BF16), NT_DIMS, preferred_element_type=F32)
            s_prev = state_ref[g]
            y_off = jnp.dot(cm_g, s_prev.astype(BF16), preferred_element_type=F32)
            y_off = y_off * sd_x[:, g * gw:(g + 1) * gw]
            for pr in range(hpg // 2):
                h0 = ci * SSD_HEADS + g * hpg + pr * 2
                lhs = jnp.concatenate([(cb * dec_ref[h0]).astype(BF16),
                                       (cb * dec_ref[h0 + 1]).astype(BF16)], axis=1)
                c0 = g * gw + pr * LANES
                xp = xs_b[:, c0:c0 + LANES]
                zero = jnp.zeros_like(xp)
                rhs = jnp.concatenate([jnp.where(left, xp, zero), jnp.where(left, zero, xp)], axis=0)
                y_parts.append(jnp.dot(lhs, rhs, preferred_element_type=F32)
                               + y_off[:, pr * LANES:(pr + 1) * LANES])
            upd = jnp.dot(bm_g.T.astype(BF16), xw_b[:, g * gw:(g + 1) * gw],
                          preferred_element_type=F32)
            state_ref[g] = s_prev * cd_x[:, g * gw:(g + 1) * gw] + upd
        y_chunks.append(jnp.concatenate(y_parts, axis=1))
    y = jnp.concatenate(y_chunks, axis=0) + xs_all * dskip_ref[...]

    yg = y * _silu(uz_ref[...].astype(F32))
    outs = []
    for g in range(SSD_GROUPS):
        v = yg[:, g * gw:(g + 1) * gw]
        ms = jnp.mean(v * v, axis=-1, keepdims=True)
        outs.append(v * lax.rsqrt(ms + EPS))
    y_ssd = jnp.concatenate(outs, axis=1) * nw_ref[...]
    o_ref[:, CONV_GROUP_W:] = y_ssd.astype(BF16)


def _mix(u, dt_raw, conv_w, wxs, wbc, bxs, bbc, dtb, alog, dskip_x, nw):
    t = u.shape[0]
    tq = MIX_TILE
    gw = CONV_GROUP_W
    bcw = 2 * SSD_GROUPS * SSD_STATE
    const = lambda i: (0, 0)
    ublk = lambda k: pl.BlockSpec((tq, gw), lambda i, k=k: (i, k))
    full = lambda arr: pl.BlockSpec(arr.shape, const)
    return pl.pallas_call(
        _mix_kernel,
        grid=(t // tq,),
        in_specs=[ublk(0), ublk(1), ublk(2), ublk(3), ublk(4),
                  pl.BlockSpec((tq, bcw), lambda i: (i, 5 * gw // bcw)),
                  pl.BlockSpec((tq, LANES), lambda i: (i, 0)),
                  full(conv_w), full(wxs), full(wbc), full(bxs), full(bbc), full(dtb), full(alog),
                  full(dskip_x), full(nw)],
        out_specs=pl.BlockSpec((tq, 2 * gw), lambda i: (i, 0)),
        out_shape=jax.ShapeDtypeStruct((t, 2 * gw), BF16),
        scratch_shapes=[pltpu.VMEM((SUBLANES, gw), F32), pltpu.VMEM((SUBLANES, gw), F32),
                        pltpu.VMEM((SUBLANES, gw), F32), pltpu.VMEM((SUBLANES, bcw), F32),
                        pltpu.VMEM((SSD_GROUPS, SSD_STATE, gw // SSD_GROUPS), F32),
                        pltpu.VMEM((tq // SSD_CHUNK * SSD_HEADS, SSD_CHUNK, SSD_CHUNK), F32),
                        pltpu.VMEM((tq, gw), F32), pltpu.VMEM((tq, bcw), F32)],
        compiler_params=_cparams(("arbitrary",)),
        name="mix",
    )(u, u, u, u, u, u, dt_raw, conv_w, wxs, wbc, bxs, bbc, dtb, alog, dskip_x, nw)


def _outproj_kernel(y_ref, x_ref, w_ref, g1_ref, nw_ref, sc_ref, sh_ref, wrh_ref, br_ref,
                    x1_ref, h_ref, rt_ref, route_ref, acc_ref, hb_ref, hl_ref, *, rows):
    tm, d = x_ref.shape
    scale = nw_ref[...] * (1.0 + sc_ref[...])
    acc_ref[...] = jnp.dot(y_ref[...], w_ref[...], preferred_element_type=F32)
    for r0 in range(0, tm, rows):
        x1 = x_ref[r0:r0 + rows, :] + g1_ref[...] * acc_ref[r0:r0 + rows, :]
        x1_ref[r0:r0 + rows, :] = x1
        ms = jnp.mean(x1 * x1, axis=-1, keepdims=True)
        h = x1 * lax.rsqrt(ms + EPS) * scale + sh_ref[...]
        h_ref[r0:r0 + rows, :] = h
        hb = h.astype(BF16)
        hb_ref[r0:r0 + rows, :] = hb
        hl_ref[r0:r0 + rows, :] = (h - hb.astype(F32)).astype(BF16)

    both = lax.dot_general(wrh_ref[...], hb_ref[...], NT_DIMS, preferred_element_type=F32)
    lt = (both[:ROUTER_ROWS, :] + both[ROUTER_ROWS:, :]
          + lax.dot_general(wrh_ref[:ROUTER_ROWS, :], hl_ref[...], NT_DIMS, preferred_element_type=F32)
          + br_ref[...])
    sub = lax.broadcasted_iota(I32, (ROUTER_ROWS, tm), 0).astype(F32)
    ninf = jnp.float32(-jnp.inf)
    big = jnp.float32(1e9)
    is_g = (sub >= N_EXPERTS) & (sub < N_EXPERTS + N_EXPERT_GROUPS)
    gl = jnp.where(is_g, lt, ninf)
    gmax = jnp.max(gl, axis=0, keepdims=True)
    grp_p = 1.0 / jnp.sum(jnp.exp(gl - gmax), axis=0, keepdims=True)
    gidx = jnp.min(jnp.where(gl == gmax, sub - N_EXPERTS, big), axis=0, keepdims=True)
    lo = gidx * EXPERTS_PER_GROUP
    el = jnp.where((sub >= lo) & (sub < lo + EXPERTS_PER_GROUP), lt, ninf)
    m1 = jnp.max(el, axis=0, keepdims=True)
    i1 = jnp.min(jnp.where(el == m1, sub, big), axis=0, keepdims=True)
    el2 = jnp.where(sub == i1, ninf, el)
    m2 = jnp.max(el2, axis=0, keepdims=True)
    i2 = jnp.min(jnp.where(el2 == m2, sub, big), axis=0, keepdims=True)
    e = jnp.exp(m2 - m1)
    w1 = grp_p / (1.0 + e)
    w2 = grp_p * e / (1.0 + e)
    rt = jnp.where(sub == 0, i1, jnp.where(sub == 1, i2,
                   jnp.where(sub == 2, w1, jnp.where(sub == 3, w2, 0.0))))
    for c, r0 in enumerate(range(0, tm, rows)):
        rt_ref[c] = rt[:SUBLANES, r0:r0 + rows]
    route_ref[...] = jnp.concatenate([rt, jnp.zeros((LANES - ROUTER_ROWS, tm), F32)], axis=0).T


def _outproj(ycat, x, w_out, g1, nw, sc, sh, wr_hilo, b_r):
    t, d = x.shape
    tm = ROW_TILE
    rows = POS_CHUNK
    const = lambda i: (0, 0)
    row = pl.BlockSpec((1, d), const)
    tile = pl.BlockSpec((tm, d), lambda i: (i, 0))
    return pl.pallas_call(
        functools.partial(_outproj_kernel, rows=rows),
        grid=(t // tm,),
        in_specs=[tile, tile,
                  pl.BlockSpec((d, d), const, pipeline_mode=pl.Buffered(1)), row, row, row, row,
                  pl.BlockSpec((2 * ROUTER_ROWS, d), const),
                  pl.BlockSpec((ROUTER_ROWS, 1), const)],
        out_specs=[tile, tile,
                   pl.BlockSpec((tm // rows, SUBLANES, rows), lambda i: (i, 0, 0)),
                   pl.BlockSpec((tm, LANES), lambda i: (i, 0))],
        out_shape=[jax.ShapeDtypeStruct((t, d), F32), jax.ShapeDtypeStruct((t, d), F32),
                   jax.ShapeDtypeStruct((t // rows, SUBLANES, rows), F32),
                   jax.ShapeDtypeStruct((t, LANES), F32)],
        scratch_shapes=[pltpu.VMEM((tm, d), F32), pltpu.VMEM((tm, d), BF16), pltpu.VMEM((tm, d), BF16)],
        compiler_params=_cparams(("arbitrary",)),
        name="outproj",
    )(ycat, x, w_out, g1, nw, sc, sh, wr_hilo, b_r)


def _meta_kernel(rt_ref, pos_ref, meta_ref):
    nchunk, _, c = rt_ref.shape
    r = float(MOE_ROWS)
    sub = lax.broadcasted_iota(I32, (LANES, c), 0).astype(F32)

    def onehots(k):
        rt = rt_ref[k]
        return (sub == rt[0:1, :]).astype(F32), (sub == rt[1:2, :]).astype(F32)

    def count(k, cnt):
        oh1, oh2 = onehots(k)
        return cnt + jnp.sum(oh1 + oh2, axis=1, keepdims=True)

    cnt = lax.fori_loop(0, nchunk, count, jnp.zeros((LANES, 1), F32))
    padded = jnp.floor((cnt + (r - 1.0)) * (1.0 / r)) * r
    er = lax.broadcasted_iota(I32, (LANES, LANES), 0)
    ec = lax.broadcasted_iota(I32, (LANES, LANES), 1)
    strict_lower = (ec < er).astype(F32)
    pstart = jnp.dot(strict_lower, jnp.broadcast_to(padded, (LANES, LANES)), precision=HIGHEST,
                     preferred_element_type=F32)[:, 0:1]
    tr = lax.broadcasted_iota(I32, (c, c), 0)
    tc = lax.broadcasted_iota(I32, (c, c), 1)
    before = (tr < tc).astype(BF16)

    def place(k, run):
        oh1, oh2 = onehots(k)
        both = oh1 + oh2
        prior = jnp.dot(both.astype(BF16), before, preferred_element_type=F32)
        slot = prior + (pstart + run)
        p1 = jnp.sum(oh1 * slot, axis=0, keepdims=True)
        p2 = jnp.sum(oh2 * slot, axis=0, keepdims=True)
        pos_ref[k] = jnp.concatenate([p1, p2], axis=0).astype(I32)
        return run + jnp.sum(both, axis=1, keepdims=True)

    lax.fori_loop(0, nchunk, place, jnp.zeros((LANES, 1), F32))

    nbl = META_LANES
    e_sub = lax.broadcasted_iota(I32, (LANES, nbl), 0).astype(F32)
    blk0 = lax.broadcasted_iota(I32, (LANES, nbl), 1).astype(F32) * r
    pend = pstart + padded
    bexp = jnp.sum(jnp.where((e_sub < N_EXPERTS) & (pend <= blk0), 1.0, 0.0), axis=0, keepdims=True)
    bexp = jnp.minimum(bexp, N_EXPERTS - 1.0)
    cend = jnp.sum(jnp.where(e_sub == bexp, pstart + cnt, 0.0), axis=0, keepdims=True)
    nvalid = jnp.clip(cend - blk0[0:1, :], 0.0, r)
    total = jnp.sum(padded, axis=0, keepdims=True)
    used = blk0[0:1, :] < total
    nvalid = jnp.where(used, nvalid, 0.0)
    pend_row = jnp.sum(jnp.where(er == ec, pend, 0.0), axis=0, keepdims=True)
    b_sub = lax.broadcasted_iota(I32, (nbl, LANES), 0).astype(F32) * r
    e_lane = lax.broadcasted_iota(I32, (nbl, LANES), 1)
    bexp_col = jnp.sum(jnp.where((e_lane < N_EXPERTS) & (pend_row <= b_sub), 1.0, 0.0),
                       axis=1, keepdims=True)
    bexp_col = jnp.minimum(bexp_col, N_EXPERTS - 1.0)
    seg_end = jnp.sum(jnp.where(e_sub == bexp, pend, 0.0), axis=0, keepdims=True)
    bp_sub = lax.broadcasted_iota(I32, (nbl, nbl), 0).astype(F32) * r
    nxt = jnp.sum(jnp.where(bp_sub == seg_end, bexp_col, 0.0), axis=0, keepdims=True)
    nxt = jnp.where(used & (seg_end < total), nxt, -1.0)
    rowsel = lax.broadcasted_iota(I32, (SUBLANES, nbl), 0)
    meta = jnp.where(rowsel == 0, bexp, jnp.where(rowsel == 1, nvalid, jnp.where(rowsel == 2, nxt, 0.0)))
    meta_ref[...] = meta.astype(I32)


def _meta(rt):
    nchunk, _, c = rt.shape
    return pl.pallas_call(
        _meta_kernel,
        out_shape=[jax.ShapeDtypeStruct((nchunk, 2, c), I32),
                   jax.ShapeDtypeStruct((SUBLANES, META_LANES), I32)],
        compiler_params=pltpu.CompilerParams(vmem_limit_bytes=VMEM_LIMIT),
        name="meta",
    )(rt)


def _dispatch_kernel(meta_ref, pos_ref, h_ref, hs_hbm, zbuf, sem, zsem):
    i = pl.program_id(0)
    tm = h_ref.shape[0]
    nck, _, c = pos_ref.shape
    r = MOE_ROWS

    @pl.when(i == 0)
    def _():
        zbuf[...] = jnp.zeros_like(zbuf)

        def zero_copy(b):
            return pltpu.make_async_copy(zbuf, hs_hbm.at[pl.ds(pl.multiple_of(b * r, r), r)], zsem)

        def zstart(b, carry):
            @pl.when(meta_ref[META_NVALID + b] < r)
            def _():
                zero_copy(b).start()
            return carry

        def zwait(b, carry):
            @pl.when(meta_ref[META_NVALID + b] < r)
            def _():
                zero_copy(b).wait()
            return carry

        lax.fori_loop(0, hs_hbm.shape[0] // r, zstart, 0)
        lax.fori_loop(0, hs_hbm.shape[0] // r, zwait, 0)

    for ck in range(nck):
        for j in range(c):
            row = ck * c + j
            for k in range(2):
                pltpu.make_async_copy(h_ref.at[pl.ds(row, 1)],
                                      hs_hbm.at[pl.ds(pos_ref[ck, k, j], 1)], sem).start(priority=k)
    for k in range(2):
        pltpu.make_async_copy(h_ref, hs_hbm.at[pl.ds(0, tm)], sem).wait()


def _dispatch(meta_flat, pos, h, n_slots):
    t, d = h.shape
    tm = ROW_TILE
    nck = tm // POS_CHUNK
    grid_spec = pltpu.PrefetchScalarGridSpec(
        num_scalar_prefetch=1,
        grid=(t // tm,),
        in_specs=[pl.BlockSpec((nck, 2, POS_CHUNK), lambda i, m: (i, 0, 0), memory_space=pltpu.SMEM),
                  pl.BlockSpec((tm, d), lambda i, m: (i, 0))],
        out_specs=pl.BlockSpec(memory_space=pl.ANY),
        scratch_shapes=[pltpu.VMEM((MOE_ROWS, d), F32), pltpu.SemaphoreType.DMA(()),
                        pltpu.SemaphoreType.DMA(())],
    )
    return pl.pallas_call(
        _dispatch_kernel,
        grid_spec=grid_spec,
        out_shape=jax.ShapeDtypeStruct((n_slots, d), F32),
        compiler_params=_cparams(("arbitrary",)),
        name="dispatch",
    )(meta_flat, pos, h)


def _moe_kernel(meta_ref, x_ref, wg_hbm, wu_hbm, wd_hbm, y_ref, stg_g, stg_u, stg_d, wgb, wub, wdb, sems):
    b = pl.program_id(0)
    e = meta_ref[META_EXPERT + b]
    nxt = meta_ref[META_NEXT + b]
    used = meta_ref[META_NVALID + b] > 0
    prev = meta_ref[META_EXPERT + jnp.maximum(b - 1, 0)]
    changed = jnp.logical_and(used, jnp.logical_or(b == 0, e != prev))
    d, f = wgb.shape
    kc = 512

    def copies(ex):
        return [pltpu.make_async_copy(w.at[ex], stg, sems.at[k])
                for k, (w, stg) in enumerate(((wg_hbm, stg_g), (wu_hbm, stg_u), (wd_hbm, stg_d)))]

    @pl.when(b == 0)
    def _():
        for cp in copies(e):
            cp.start()

    def convert_dot(lhs, stg, wb, n):
        acc = None
        for k0 in range(0, n, kc):
            w = stg[k0:k0 + kc, :].astype(BF16)
            wb[k0:k0 + kc, :] = w
            part = jnp.dot(lhs[:, k0:k0 + kc], w, preferred_element_type=F32)
            acc = part if acc is None else acc + part
        return acc

    @pl.when(changed)
    def _():
        cps = copies(e)
        x = x_ref[...].astype(BF16)
        cps[0].wait()
        gate = convert_dot(x, stg_g, wgb, d)
        cps[1].wait()
        up = convert_dot(x, stg_u, wub, d)
        hid = (_silu(gate) * up).astype(BF16)
        cps[2].wait()
        y_ref[...] = convert_dot(hid, stg_d, wdb, f)

        @pl.when(nxt >= 0)
        def _():
            for cp in copies(nxt):
                cp.start(priority=1)

    @pl.when(jnp.logical_and(used, jnp.logical_not(changed)))
    def _():
        x = x_ref[...].astype(BF16)
        gate = jnp.dot(x, wgb[...], preferred_element_type=F32)
        up = jnp.dot(x, wub[...], preferred_element_type=F32)
        hid = (_silu(gate) * up).astype(BF16)
        y_ref[...] = jnp.dot(hid, wdb[...], preferred_element_type=F32)

    @pl.when(jnp.logical_not(used))
    def _():
        y_ref[...] = jnp.zeros_like(y_ref)


def _moe(meta_flat, hs, w_gate, w_up, w_down):
    n_slots, d = hs.shape
    f = w_gate.shape[2]
    hbm = pl.BlockSpec(memory_space=pl.ANY)
    grid_spec = pltpu.PrefetchScalarGridSpec(
        num_scalar_prefetch=1,
        grid=(n_slots // MOE_ROWS,),
        in_specs=[pl.BlockSpec((MOE_ROWS, d), lambda b, m: (b, 0)), hbm, hbm, hbm],
        out_specs=pl.BlockSpec((MOE_ROWS, d), lambda b, m: (b, 0)),
        scratch_shapes=[pltpu.VMEM((d, f), F32), pltpu.VMEM((d, f), F32), pltpu.VMEM((f, d), F32),
                        pltpu.VMEM((d, f), BF16), pltpu.VMEM((d, f), BF16), pltpu.VMEM((f, d), BF16),
                        pltpu.SemaphoreType.DMA((3,))],
    )
    return pl.pallas_call(
        _moe_kernel,
        grid_spec=grid_spec,
        out_shape=jax.ShapeDtypeStruct((n_slots, d), F32),
        compiler_params=_cparams(("arbitrary",)),
        name="moe",
    )(meta_flat, hs, w_gate, w_up, w_down)


def _final_kernel(pos_ref, posn_ref, y_hbm, x1_ref, route_ref, g2_ref, nw_ref, o_ref, ybuf, sems):
    i = pl.program_id(0)
    nt = pl.num_programs(0)
    tm = x1_ref.shape[0]
    slot = lax.rem(i, 2)

    def issue(p_ref, s):
        for j in range(tm):
            for k in range(2):
                pltpu.make_async_copy(y_hbm.at[pl.ds(p_ref[0, k, j], 1)],
                                      ybuf.at[s, k, pl.ds(j, 1)], sems.at[s]).start(priority=k)

    @pl.when(i == 0)
    def _():
        issue(pos_ref, 0)

    for s in range(2):
        @pl.when(jnp.logical_and(i + 1 < nt, slot == 1 - s))
        def _(s=s):
            issue(posn_ref, s)

    for k in range(2):
        pltpu.make_async_copy(y_hbm.at[pl.ds(0, tm)], ybuf.at[slot, k], sems.at[slot]).wait()

    rt = route_ref[...]
    moe = rt[:, 2:3] * ybuf[slot, 0] + rt[:, 3:4] * ybuf[slot, 1]
    x2 = x1_ref[...] + g2_ref[...] * moe
    ms = jnp.mean(x2 * x2, axis=-1, keepdims=True)
    o_ref[...] = x2 * lax.rsqrt(ms + EPS) * nw_ref[...]


def _final(y_sorted, pos, x1, route, g2, nw):
    t, d = x1.shape
    tm = POS_CHUNK
    nt = t // tm
    const = lambda i: (0, 0)
    return pl.pallas_call(
        _final_kernel,
        grid=(nt,),
        in_specs=[pl.BlockSpec((1, 2, tm), lambda i: (i, 0, 0), memory_space=pltpu.SMEM),
                  pl.BlockSpec((1, 2, tm), lambda i: (jnp.minimum(i + 1, nt - 1), 0, 0),
                               memory_space=pltpu.SMEM),
                  pl.BlockSpec(memory_space=pl.ANY),
                  pl.BlockSpec((tm, d), lambda i: (i, 0)),
                  pl.BlockSpec((tm, LANES), lambda i: (i, 0)),
                  pl.BlockSpec((1, d), const), pl.BlockSpec((1, d), const)],
        out_specs=pl.BlockSpec((tm, d), lambda i: (i, 0)),
        out_shape=jax.ShapeDtypeStruct((t, d), F32),
        scratch_shapes=[pltpu.VMEM((2, 2, tm, d), F32), pltpu.SemaphoreType.DMA((2,))],
        compiler_params=_cparams(("arbitrary",)),
        name="final",
    )(pos, pos, y_sorted, x1, route, g2, nw)


def kernel(x, c, w_ada, b_ada, norm1_w, w_in, conv_w, ssd_conv_w, ssd_conv_b, dt_bias, a_log, d_skip,
           ssd_norm_w, w_out, norm2_w, w_router_grp, b_router_grp, w_router_exp, b_router_exp,
           w_gate, w_up, w_down, final_norm_w):
    bsz, seq, d = x.shape
    assert bsz == 1 and w_ada.shape[0] == 1
    t = bsz * seq
    assert t % ROW_TILE == 0
    xt = x.reshape(t, d)
    gw = CONV_GROUP_W
    n_main = 5 * gw + 2 * SSD_GROUPS * SSD_STATE
    n_heads = w_in.shape[2] - n_main
    assert n_heads == SSD_HEADS

    mod = _ada(c, w_ada[0], b_ada[0])
    sh1, sc1, g1, sh2, sc2, g2 = [mod[:, k * d:(k + 1) * d] for k in range(6)]

    w_all = w_in[0].astype(BF16)
    w_dt = jnp.pad(w_all[:, n_main:], ((0, 0), (0, LANES - n_heads)))
    u, dt_raw = _inproj(xt, norm1_w, sc1, sh1, w_all, w_dt, n_main)

    pad_h = lambda v: jnp.pad(v, ((0, 0), (0, LANES - n_heads)))
    ycat = _mix(u, dt_raw, conv_w[0], ssd_conv_w[0][:, :gw], ssd_conv_w[0][:, gw:],
                ssd_conv_b[:, :gw], ssd_conv_b[:, gw:], pad_h(dt_bias), pad_h(a_log),
                jnp.repeat(d_skip, SSD_HEAD_DIM, axis=1), ssd_norm_w)

    n_r = N_EXPERTS + N_EXPERT_GROUPS
    w_rt = jnp.pad(jnp.concatenate([w_router_exp[0], w_router_grp[0]], axis=1).T,
                   ((0, ROUTER_ROWS - n_r), (0, 0)))
    wr_hi = w_rt.astype(BF16)
    wr_lo = (w_rt - wr_hi.astype(F32)).astype(BF16)
    b_r = jnp.pad(jnp.concatenate([b_router_exp, b_router_grp], axis=1),
                  ((0, 0), (0, ROUTER_ROWS - n_r))).reshape(ROUTER_ROWS, 1)
    x1, h2, rt, route = _outproj(ycat, xt, w_out[0].astype(BF16), g1, norm2_w, sc2, sh2,
                                 jnp.concatenate([wr_hi, wr_lo], axis=0), b_r)

    pos, meta = _meta(rt)
    meta_flat = meta.reshape(SUBLANES * META_LANES)
    n_blocks = (2 * t + N_EXPERTS * (MOE_ROWS - 1) + MOE_ROWS - 1) // MOE_ROWS
    assert n_blocks <= META_LANES
    hs = _dispatch(meta_flat, pos, h2, n_blocks * MOE_ROWS)
    y_sorted = _moe(meta_flat, hs, w_gate[0], w_up[0], w_down[0])
    out = _final(y_sorted, pos, x1, route, g2, final_norm_w.reshape(1, d))
    return out.reshape(bsz, seq, d)
```

```python
import functools

import jax
import jax.numpy as jnp
from jax import lax
from jax.experimental import pallas as pl
from jax.experimental.pallas import tpu as pltpu

F32 = jnp.float32
BF16 = jnp.bfloat16
U32 = jnp.uint32
I32 = jnp.int32
EPS = 1e-6

CONV_GROUP_W = 1024
SSD_HEADS = 16
SSD_HEAD_DIM = 64
SSD_GROUPS = 2
SSD_STATE = 128
N_EXPERT_GROUPS = 4
EXPERTS_PER_GROUP = 8
N_EXPERTS = 32
LANES = 128
SUBLANES = 8
VMEM_LIMIT = 56 * 1024 * 1024

SSD_CHUNK = 128
MIX_TILE = 256
CONV_CHUNK = 256
ROW_TILE = 512
DISPATCH_TILE = 1024
POS_CHUNK = 256
MOE_ROWS = 256
ROUTER_ROWS = 48
META_LANES = 256
META_EXPERT, META_NVALID, META_NEXT = 0, META_LANES, 2 * META_LANES
HIGHEST = lax.Precision.HIGHEST
NT_DIMS = (((1,), (1,)), ((), ()))


def _silu(v):
    return 0.5 * v * (1.0 + jnp.tanh(0.5 * v))


def _cparams(sem):
    return pltpu.CompilerParams(dimension_semantics=sem, vmem_limit_bytes=VMEM_LIMIT)


def _ada_kernel(c_ref, w_ref, b_ref, o_ref):
    s = _silu(c_ref[...])
    o_ref[...] = jnp.sum(w_ref[...] * s, axis=0, keepdims=True) + b_ref[...]


def _ada(c, w_ada, b_ada):
    d, n = w_ada.shape
    tn = 1024
    return pl.pallas_call(
        _ada_kernel,
        grid=(n // tn,),
        in_specs=[pl.BlockSpec((d, 1), lambda j: (0, 0)),
                  pl.BlockSpec((d, tn), lambda j: (0, j)),
                  pl.BlockSpec((1, tn), lambda j: (0, j))],
        out_specs=pl.BlockSpec((1, tn), lambda j: (0, j)),
        out_shape=jax.ShapeDtypeStruct((1, n), F32),
        compiler_params=_cparams(("arbitrary",)),
        name="ada",
    )(c.reshape(d, 1), w_ada, b_ada.reshape(1, n))


def _inproj_kernel(x_ref, nw_ref, sc_ref, sh_ref, w_ref, wdt_ref, u_ref, dt_ref, h_ref, *, rows, tn):
    tm = x_ref.shape[0]
    n = w_ref.shape[1]
    scale = nw_ref[...] * (1.0 + sc_ref[...])
    for r0 in range(0, tm, rows):
        xf = x_ref[r0:r0 + rows, :]
        ms = jnp.mean(xf * xf, axis=-1, keepdims=True)
        h_ref[r0:r0 + rows, :] = (xf * lax.rsqrt(ms + EPS) * scale + sh_ref[...]).astype(BF16)
        h = h_ref[r0:r0 + rows, :]
        dt_ref[r0:r0 + rows, :] = jnp.dot(h, wdt_ref[...], preferred_element_type=F32)
        for n0 in range(0, n, tn):
            u_ref[r0:r0 + rows, n0:n0 + tn] = jnp.dot(
                h, w_ref[:, n0:n0 + tn], preferred_element_type=F32).astype(BF16)


def _inproj(x, nw, sc, sh, w_all, w_dt, n):
    t, d = x.shape
    tm = ROW_TILE
    const = lambda i: (0, 0)
    row = pl.BlockSpec((1, d), const)
    return pl.pallas_call(
        functools.partial(_inproj_kernel, rows=256, tn=512),
        grid=(t // tm,),
        in_specs=[pl.BlockSpec((tm, d), lambda i: (i, 0)), row, row, row,
                  pl.BlockSpec((d, n), const, pipeline_mode=pl.Buffered(1)),
                  pl.BlockSpec((d, LANES), const)],
        out_specs=[pl.BlockSpec((tm, n), lambda i: (i, 0)),
                   pl.BlockSpec((tm, LANES), lambda i: (i, 0))],
        out_shape=[jax.ShapeDtypeStruct((t, n), BF16), jax.ShapeDtypeStruct((t, LANES), F32)],
        scratch_shapes=[pltpu.VMEM((tm, d), BF16)],
        compiler_params=_cparams(("arbitrary",)),
        name="inproj",
    )(x, nw, sc, sh, w_all, w_dt)


def _shift_matrix(q, nshift):
    rr = lax.broadcasted_iota(I32, (q, q), 0)
    cc = lax.broadcasted_iota(I32, (q, q), 1)
    return jnp.concatenate([(rr - cc == s).astype(BF16) for s in range(1, nshift + 1)], axis=0)


def _shifted_rows(smat, x_b, x_f, tail_ref, cols, nshift):
    q = x_b.shape[0]
    sh = jnp.dot(smat[:nshift * q, :], x_b, preferred_element_type=F32)
    head = jnp.concatenate([tail_ref[:, cols], x_f[0:SUBLANES, :]], axis=0)
    outs = [jnp.concatenate([head[SUBLANES - s:2 * SUBLANES - s, :],
                             sh[(s - 1) * q + SUBLANES:s * q, :]], axis=0)
            for s in range(1, nshift + 1)]
    tail_ref[:, cols] = x_f[q - SUBLANES:, :]
    return outs


def _conv4_silu(smat, x_ref, tail_ref, w_ref, b_ref, out_ref):
    for c0 in range(0, x_ref.shape[1], CONV_CHUNK):
        cols = slice(c0, c0 + CONV_CHUNK)
        x_b = x_ref[:, cols]
        x_f = x_b.astype(F32)
        s1, s2, s3 = _shifted_rows(smat, x_b, x_f, tail_ref, cols, 3)
        conv = (w_ref[3:4, cols] * x_f + w_ref[2:3, cols] * s1 + w_ref[1:2, cols] * s2
                + w_ref[0:1, cols] * s3)
        out_ref[:, cols] = _silu(conv + b_ref[:, cols])


def _expand_heads(v, left):
    parts = [jnp.where(left, v[:, 2 * k:2 * k + 1], v[:, 2 * k + 1:2 * k + 2])
             for k in range(SSD_HEADS // 2)]
    return jnp.concatenate(parts, axis=1)


def _mix_kernel(ub_ref, uc_ref, uv_ref, uz_ref, uxs_ref, ubc_ref, dt_ref,
                cw_ref, wxs_ref, wbc_ref, bxs_ref, bbc_ref, dtb_ref, alog_ref, dskip_ref,
                nw_ref, o_ref,
                tail_c, tail_v, tail_xs, tail_bc, state_ref, dec_ref, xs_ref, bc_ref):
    i = pl.program_id(0)
    tq = ub_ref.shape[0]
    q = SSD_CHUNK
    gw = SSD_HEADS // SSD_GROUPS * SSD_HEAD_DIM
    hpg = SSD_HEADS // SSD_GROUPS

    @pl.when(i == 0)
    def _():
        for tail in (tail_c, tail_v, tail_xs, tail_bc):
            tail[...] = jnp.zeros_like(tail)
        state_ref[...] = jnp.zeros_like(state_ref)

    dtv = dt_ref[...] + dtb_ref[...]
    dt_all = jnp.maximum(dtv, 0.0) + jnp.log1p(jnp.exp(-jnp.abs(dtv)))
    a_all = dt_all * (-jnp.exp(alog_ref[...]))
    rr = lax.broadcasted_iota(I32, (q, q), 0)
    cc = lax.broadcasted_iota(I32, (q, q), 1)
    causal = rr >= cc
    tri = causal.astype(F32)
    left = lax.broadcasted_iota(I32, (q, LANES), 1) < SSD_HEAD_DIM
    decays = []
    for ci, r0 in enumerate(range(0, tq, q)):
        dt = dt_all[r0:r0 + q, :]
        a_cs = jnp.dot(tri, a_all[r0:r0 + q, :], precision=HIGHEST,
                       preferred_element_type=F32)
        a_cs_t = a_cs.T
        dt_t = dt.T
        last = a_cs[q - 1:q, :]
        sd_x = _expand_heads(jnp.exp(a_cs), left)
        w_x = _expand_heads(dt * jnp.exp(last - a_cs), left)
        cd_x = _expand_heads(jnp.exp(last), left[0:1, :])
        decays.append((sd_x, w_x, cd_x))
        for h in range(SSD_HEADS):
            decay = jnp.exp(a_cs[:, h:h + 1] - a_cs_t[h:h + 1, :])
            dec_ref[ci * SSD_HEADS + h] = jnp.where(causal, decay * dt_t[h:h + 1, :], 0.0)

    smat = _shift_matrix(tq, 3)

    for c0 in range(0, CONV_GROUP_W, CONV_CHUNK):
        cols = slice(c0, c0 + CONV_CHUNK)
        c_b, v_b = uc_ref[:, cols], uv_ref[:, cols]
        c_f, v_f = c_b.astype(F32), v_b.astype(F32)
        c1, c2 = _shifted_rows(smat, c_b, c_f, tail_c, cols, 2)
        v1, v2 = _shifted_rows(smat, v_b, v_f, tail_v, cols, 2)
        conv = (cw_ref[2:3, cols] * (c_f * v_f) + cw_ref[1:2, cols] * (c1 * v1)
                + cw_ref[0:1, cols] * (c2 * v2))
        o_ref[:, cols] = (ub_ref[:, cols].astype(F32) * conv).astype(BF16)

    _conv4_silu(smat, uxs_ref, tail_xs, wxs_ref, bxs_ref, xs_ref)
    _conv4_silu(smat, ubc_ref, tail_bc, wbc_ref, bbc_ref, bc_ref)
    xs_all = xs_ref[...]
    bc_all = bc_ref[...]
    gs = SSD_GROUPS * SSD_STATE

    y_chunks = []
    for ci, r0 in enumerate(range(0, tq, q)):
        xs = xs_all[r0:r0 + q, :]
        bm = bc_all[r0:r0 + q, :gs]
        cm = bc_all[r0:r0 + q, gs:]
        sd_x, w_x, cd_x = decays[ci]
        xs_b = xs.astype(BF16)
        xw_b = (xs * w_x).astype(BF16)

        y_parts = []
        for g in range(SSD_GROUPS):
            cm_g = cm[:, g * SSD_STATE:(g + 1) * SSD_STATE].astype(BF16)
            bm_g = bm[:, g * SSD_STATE:(g + 1) * SSD_STATE]
            cb = lax.dot_general(cm_g, bm_g.astype(BF16), NT_DIMS, preferred_element_type=F32)
            s_prev = state_ref[g]
            y_off = jnp.dot(cm_g, s_prev.astype(BF16), preferred_element_type=F32)
            y_off = y_off * sd_x[:, g * gw:(g + 1) * gw]
            for pr in range(hpg // 2):
                h0 = ci * SSD_HEADS + g * hpg + pr * 2
                lhs = jnp.concatenate([(cb * dec_ref[h0]).astype(BF16),
                                       (cb * dec_ref[h0 + 1]).astype(BF16)], axis=1)
                c0 = g * gw + pr * LANES
                xp = xs_b[:, c0:c0 + LANES]
                zero = jnp.zeros_like(xp)
                rhs = jnp.concatenate([jnp.where(left, xp, zero), jnp.where(left, zero, xp)], axis=0)
                y_parts.append(jnp.dot(lhs, rhs, preferred_element_type=F32)
                               + y_off[:, pr * LANES:(pr + 1) * LANES])
            upd = jnp.dot(bm_g.T.astype(BF16), xw_b[:, g * gw:(g + 1) * gw],
                          preferred_element_type=F32)
            state_ref[g] = s_prev * cd_x[:, g * gw:(g + 1) * gw] + upd
        y_chunks.append(jnp.concatenate(y_parts, axis=1))
    y = jnp.concatenate(y_chunks, axis=0) + xs_all * dskip_ref[...]

    yg = y * _silu(uz_ref[...].astype(F32))
    outs = []
    for g in range(SSD_GROUPS):
        v = yg[:, g * gw:(g + 1) * gw]
        ms = jnp.mean(v * v, axis=-1, keepdims=True)
        outs.append(v * lax.rsqrt(ms + EPS))
    y_ssd = jnp.concatenate(outs, axis=1) * nw_ref[...]
    o_ref[:, CONV_GROUP_W:] = y_ssd.astype(BF16)


def _mix(u, dt_raw, conv_w, wxs, wbc, bxs, bbc, dtb, alog, dskip_x, nw):
    t = u.shape[0]
    tq = MIX_TILE
    gw = CONV_GROUP_W
    bcw = 2 * SSD_GROUPS * SSD_STATE
    const = lambda i: (0, 0)
    ublk = lambda k: pl.BlockSpec((tq, gw), lambda i, k=k: (i, k))
    full = lambda arr: pl.BlockSpec(arr.shape, const)
    return pl.pallas_call(
        _mix_kernel,
        grid=(t // tq,),
        in_specs=[ublk(0), ublk(1), ublk(2), ublk(3), ublk(4),
                  pl.BlockSpec((tq, bcw), lambda i: (i, 5 * gw // bcw)),
                  pl.BlockSpec((tq, LANES), lambda i: (i, 0)),
                  full(conv_w), full(wxs), full(wbc), full(bxs), full(bbc), full(dtb), full(alog),
                  full(dskip_x), full(nw)],
        out_specs=pl.BlockSpec((tq, 2 * gw), lambda i: (i, 0)),
        out_shape=jax.ShapeDtypeStruct((t, 2 * gw), BF16),
        scratch_shapes=[pltpu.VMEM((SUBLANES, gw), F32), pltpu.VMEM((SUBLANES, gw), F32),
                        pltpu.VMEM((SUBLANES, gw), F32), pltpu.VMEM((SUBLANES, bcw), F32),
                        pltpu.VMEM((SSD_GROUPS, SSD_STATE, gw // SSD_GROUPS), F32),
                        pltpu.VMEM((tq // SSD_CHUNK * SSD_HEADS, SSD_CHUNK, SSD_CHUNK), F32),
                        pltpu.VMEM((tq, gw), F32), pltpu.VMEM((tq, bcw), F32)],
        compiler_params=_cparams(("arbitrary",)),
        name="mix",
    )(u, u, u, u, u, u, dt_raw, conv_w, wxs, wbc, bxs, bbc, dtb, alog, dskip_x, nw)


def _outproj_kernel(y_ref, x_ref, w_ref, g1_ref, nw_ref, sc_ref, sh_ref, wrh_ref, br_ref,
                    x1_ref, h_ref, rt_ref, route_ref, acc_ref, hb_ref, hl_ref, *, rows):
    tm, d = x_ref.shape
    scale = nw_ref[...] * (1.0 + sc_ref[...])
    acc_ref[...] = jnp.dot(y_ref[...], w_ref[...], preferred_element_type=F32)
    for r0 in range(0, tm, rows):
        x1 = x_ref[r0:r0 + rows, :] + g1_ref[...] * acc_ref[r0:r0 + rows, :]
        x1_ref[r0:r0 + rows, :] = x1
        ms = jnp.mean(x1 * x1, axis=-1, keepdims=True)
        h = x1 * lax.rsqrt(ms + EPS) * scale + sh_ref[...]
        h_ref[r0:r0 + rows, :] = h
        hb = h.astype(BF16)
        hb_ref[r0:r0 + rows, :] = hb
        hl_ref[r0:r0 + rows, :] = (h - hb.astype(F32)).astype(BF16)

    both = lax.dot_general(wrh_ref[...], hb_ref[...], NT_DIMS, preferred_element_type=F32)
    lt = (both[:ROUTER_ROWS, :] + both[ROUTER_ROWS:, :]
          + lax.dot_general(wrh_ref[:ROUTER_ROWS, :], hl_ref[...], NT_DIMS, preferred_element_type=F32)
          + br_ref[...])
    sub = lax.broadcasted_iota(I32, (ROUTER_ROWS, tm), 0).astype(F32)
    ninf = jnp.float32(-jnp.inf)
    big = jnp.float32(1e9)
    is_g = (sub >= N_EXPERTS) & (sub < N_EXPERTS + N_EXPERT_GROUPS)
    gl = jnp.where(is_g, lt, ninf)
    gmax = jnp.max(gl, axis=0, keepdims=True)
    grp_p = 1.0 / jnp.sum(jnp.exp(gl - gmax), axis=0, keepdims=True)
    gidx = jnp.min(jnp.where(gl == gmax, sub - N_EXPERTS, big), axis=0, keepdims=True)
    lo = gidx * EXPERTS_PER_GROUP
    el = jnp.where((sub >= lo) & (sub < lo + EXPERTS_PER_GROUP), lt, ninf)
    m1 = jnp.max(el, axis=0, keepdims=True)
    i1 = jnp.min(jnp.where(el == m1, sub, big), axis=0, keepdims=True)
    el2 = jnp.where(sub == i1, ninf, el)
    m2 = jnp.max(el2, axis=0, keepdims=True)
    i2 = jnp.min(jnp.where(el2 == m2, sub, big), axis=0, keepdims=True)
    e = jnp.exp(m2 - m1)
    w1 = grp_p / (1.0 + e)
    w2 = grp_p * e / (1.0 + e)
    rt = jnp.where(sub == 0, i1, jnp.where(sub == 1, i2,
                   jnp.where(sub == 2, w1, jnp.where(sub == 3, w2, 0.0))))
    for c, r0 in enumerate(range(0, tm, rows)):
        rt_ref[c] = rt[:SUBLANES, r0:r0 + rows]
    route_ref[...] = jnp.concatenate([rt, jnp.zeros((LANES - ROUTER_ROWS, tm), F32)], axis=0).T


def _outproj(ycat, x, w_out, g1, nw, sc, sh, wr_hilo, b_r):
    t, d = x.shape
    tm = ROW_TILE
    rows = POS_CHUNK
    const = lambda i: (0, 0)
    row = pl.BlockSpec((1, d), const)
    tile = pl.BlockSpec((tm, d), lambda i: (i, 0))
    return pl.pallas_call(
        functools.partial(_outproj_kernel, rows=rows),
        grid=(t // tm,),
        in_specs=[tile, tile,
                  pl.BlockSpec((d, d), const, pipeline_mode=pl.Buffered(1)), row, row, row, row,
                  pl.BlockSpec((2 * ROUTER_ROWS, d), const),
                  pl.BlockSpec((ROUTER_ROWS, 1), const)],
        out_specs=[tile, tile,
                   pl.BlockSpec((tm // rows, SUBLANES, rows), lambda i: (i, 0, 0)),
                   pl.BlockSpec((tm, LANES), lambda i: (i, 0))],
        out_shape=[jax.ShapeDtypeStruct((t, d), F32), jax.ShapeDtypeStruct((t, d), F32),
                   jax.ShapeDtypeStruct((t // rows, SUBLANES, rows), F32),
                   jax.ShapeDtypeStruct((t, LANES), F32)],
        scratch_shapes=[pltpu.VMEM((tm, d), F32), pltpu.VMEM((tm, d), BF16), pltpu.VMEM((tm, d), BF16)],
        compiler_params=_cparams(("arbitrary",)),
        name="outproj",
    )(ycat, x, w_out, g1, nw, sc, sh, wr_hilo, b_r)


def _meta_kernel(rt_ref, pos_ref, meta_ref):
    nchunk, _, c = rt_ref.shape
    r = float(MOE_ROWS)
    sub = lax.broadcasted_iota(I32, (LANES, c), 0).astype(F32)

    def onehots(k):
        rt = rt_ref[k]
        return (sub == rt[0:1, :]).astype(F32), (sub == rt[1:2, :]).astype(F32)

    def count(k, cnt):
        oh1, oh2 = onehots(k)
        return cnt + jnp.sum(oh1 + oh2, axis=1, keepdims=True)

    cnt = lax.fori_loop(0, nchunk, count, jnp.zeros((LANES, 1), F32))
    padded = jnp.floor((cnt + (r - 1.0)) * (1.0 / r)) * r
    er = lax.broadcasted_iota(I32, (LANES, LANES), 0)
    ec = lax.broadcasted_iota(I32, (LANES, LANES), 1)
    strict_lower = (ec < er).astype(F32)
    pstart = jnp.dot(strict_lower, jnp.broadcast_to(padded, (LANES, LANES)), precision=HIGHEST,
                     preferred_element_type=F32)[:, 0:1]
    tr = lax.broadcasted_iota(I32, (c, c), 0)
    tc = lax.broadcasted_iota(I32, (c, c), 1)
    before = (tr < tc).astype(BF16)

    def place(k, run):
        oh1, oh2 = onehots(k)
        both = oh1 + oh2
        prior = jnp.dot(both.astype(BF16), before, preferred_element_type=F32)
        slot = prior + (pstart + run)
        p1 = jnp.sum(oh1 * slot, axis=0, keepdims=True)
        p2 = jnp.sum(oh2 * slot, axis=0, keepdims=True)
        pos_ref[k] = jnp.concatenate([p1, p2], axis=0).astype(I32)
        return run + jnp.sum(both, axis=1, keepdims=True)

    lax.fori_loop(0, nchunk, place, jnp.zeros((LANES, 1), F32))

    nbl = META_LANES
    e_sub = lax.broadcasted_iota(I32, (LANES, nbl), 0).astype(F32)
    blk0 = lax.broadcasted_iota(I32, (LANES, nbl), 1).astype(F32) * r
    pend = pstart + padded
    bexp = jnp.sum(jnp.where((e_sub < N_EXPERTS) & (pend <= blk0), 1.0, 0.0), axis=0, keepdims=True)
    bexp = jnp.minimum(bexp, N_EXPERTS - 1.0)
    cend = jnp.sum(jnp.where(e_sub == bexp, pstart + cnt, 0.0), axis=0, keepdims=True)
    nvalid = jnp.clip(cend - blk0[0:1, :], 0.0, r)
    total = jnp.sum(padded, axis=0, keepdims=True)
    used = blk0[0:1, :] < total
    nvalid = jnp.where(used, nvalid, 0.0)
    pend_row = jnp.sum(jnp.where(er == ec, pend, 0.0), axis=0, keepdims=True)
    b_sub = lax.broadcasted_iota(I32, (nbl, LANES), 0).astype(F32) * r
    e_lane = lax.broadcasted_iota(I32, (nbl, LANES), 1)
    bexp_col = jnp.sum(jnp.where((e_lane < N_EXPERTS) & (pend_row <= b_sub), 1.0, 0.0),
                       axis=1, keepdims=True)
    bexp_col = jnp.minimum(bexp_col, N_EXPERTS - 1.0)
    seg_end = jnp.sum(jnp.where(e_sub == bexp, pend, 0.0), axis=0, keepdims=True)
    bp_sub = lax.broadcasted_iota(I32, (nbl, nbl), 0).astype(F32) * r
    nxt = jnp.sum(jnp.where(bp_sub == seg_end, bexp_col, 0.0), axis=0, keepdims=True)
    nxt = jnp.where(used & (seg_end < total), nxt, -1.0)
    rowsel = lax.broadcasted_iota(I32, (SUBLANES, nbl), 0)
    meta = jnp.where(rowsel == 0, bexp, jnp.where(rowsel == 1, nvalid, jnp.where(rowsel == 2, nxt, 0.0)))
    meta_ref[...] = meta.astype(I32)


def _meta(rt):
    nchunk, _, c = rt.shape
    return pl.pallas_call(
        _meta_kernel,
        out_shape=[jax.ShapeDtypeStruct((nchunk, 2, c), I32),
                   jax.ShapeDtypeStruct((SUBLANES, META_LANES), I32)],
        compiler_params=pltpu.CompilerParams(vmem_limit_bytes=VMEM_LIMIT),
        name="meta",
    )(rt)


def _dispatch_kernel(meta_ref, pos_ref, h_ref, hs_hbm, zbuf, sem, zsem):
    i = pl.program_id(0)
    tm = h_ref.shape[0]
    nck, _, c = pos_ref.shape
    r = MOE_ROWS

    @pl.when(i == 0)
    def _():
        zbuf[...] = jnp.zeros_like(zbuf)

        def zero_copy(b):
            return pltpu.make_async_copy(zbuf, hs_hbm.at[pl.ds(pl.multiple_of(b * r, r), r)], zsem)

        def zstart(b, carry):
            @pl.when(meta_ref[META_NVALID + b] < r)
            def _():
                zero_copy(b).start()
            return carry

        def zwait(b, carry):
            @pl.when(meta_ref[META_NVALID + b] < r)
            def _():
                zero_copy(b).wait()
            return carry

        lax.fori_loop(0, hs_hbm.shape[0] // r, zstart, 0)
        lax.fori_loop(0, hs_hbm.shape[0] // r, zwait, 0)

    for ck in range(nck):
        for j in range(c):
            row = ck * c + j
            for k in range(2):
                pltpu.make_async_copy(h_ref.at[pl.ds(row, 1)],
                                      hs_hbm.at[pl.ds(pos_ref[ck, k, j], 1)], sem).start(priority=k)
    for k in range(2):
        pltpu.make_async_copy(h_ref, hs_hbm.at[pl.ds(0, tm)], sem).wait()


def _dispatch(meta_flat, pos, h, n_slots):
    t, d = h.shape
    tm = DISPATCH_TILE
    nck = tm // POS_CHUNK
    grid_spec = pltpu.PrefetchScalarGridSpec(
        num_scalar_prefetch=1,
        grid=(t // tm,),
        in_specs=[pl.BlockSpec((nck, 2, POS_CHUNK), lambda i, m: (i, 0, 0), memory_space=pltpu.SMEM),
                  pl.BlockSpec((tm, d), lambda i, m: (i, 0))],
        out_specs=pl.BlockSpec(memory_space=pl.ANY),
        scratch_shapes=[pltpu.VMEM((MOE_ROWS, d), F32), pltpu.SemaphoreType.DMA(()),
                        pltpu.SemaphoreType.DMA(())],
    )
    return pl.pallas_call(
        _dispatch_kernel,
        grid_spec=grid_spec,
        out_shape=jax.ShapeDtypeStruct((n_slots, d), F32),
        compiler_params=_cparams(("arbitrary",)),
        name="dispatch",
    )(meta_flat, pos, h)


def _moe_kernel(meta_ref, x_ref, wg_hbm, wu_hbm, wd_hbm, y_ref, stg_g, stg_u, stg_d, wgb, wub, wdb, sems):
    b = pl.program_id(0)
    e = meta_ref[META_EXPERT + b]
    nxt = meta_ref[META_NEXT + b]
    used = meta_ref[META_NVALID + b] > 0
    prev = meta_ref[META_EXPERT + jnp.maximum(b - 1, 0)]
    changed = jnp.logical_and(used, jnp.logical_or(b == 0, e != prev))
    d, f = wgb.shape
    kc = 512

    def copies(ex):
        return [pltpu.make_async_copy(w.at[ex], stg, sems.at[k])
                for k, (w, stg) in enumerate(((wg_hbm, stg_g), (wu_hbm, stg_u), (wd_hbm, stg_d)))]

    @pl.when(b == 0)
    def _():
        for cp in copies(e):
            cp.start()

    def convert_dot(lhs, stg, wb, n):
        acc = None
        for k0 in range(0, n, kc):
            w = stg[k0:k0 + kc, :].astype(BF16)
            wb[k0:k0 + kc, :] = w
            part = jnp.dot(lhs[:, k0:k0 + kc], w, preferred_element_type=F32)
            acc = part if acc is None else acc + part
        return acc

    @pl.when(changed)
    def _():
        cps = copies(e)
        x = x_ref[...].astype(BF16)
        cps[0].wait()
        gate = convert_dot(x, stg_g, wgb, d)
        cps[1].wait()
        up = convert_dot(x, stg_u, wub, d)
        hid = (_silu(gate) * up).astype(BF16)
        cps[2].wait()
        y_ref[...] = convert_dot(hid, stg_d, wdb, f)

        @pl.when(nxt >= 0)
        def _():
            for cp in copies(nxt):
                cp.start(priority=1)

    @pl.when(jnp.logical_and(used, jnp.logical_not(changed)))
    def _():
        x = x_ref[...].astype(BF16)
        gate = jnp.dot(x, wgb[...], preferred_element_type=F32)
        up = jnp.dot(x, wub[...], preferred_element_type=F32)
        hid = (_silu(gate) * up).astype(BF16)
        y_ref[...] = jnp.dot(hid, wdb[...], preferred_element_type=F32)

    @pl.when(jnp.logical_not(used))
    def _():
        y_ref[...] = jnp.zeros_like(y_ref)


def _moe(meta_flat, hs, w_gate, w_up, w_down):
    n_slots, d = hs.shape
    f = w_gate.shape[2]
    hbm = pl.BlockSpec(memory_space=pl.ANY)
    grid_spec = pltpu.PrefetchScalarGridSpec(
        num_scalar_prefetch=1,
        grid=(n_slots // MOE_ROWS,),
        in_specs=[pl.BlockSpec((MOE_ROWS, d), lambda b, m: (jnp.where(m[META_NVALID + b] > 0, b, 0), 0)),
                  hbm, hbm, hbm],
        out_specs=pl.BlockSpec((MOE_ROWS, d), lambda b, m: (b, 0)),
        scratch_shapes=[pltpu.VMEM((d, f), F32), pltpu.VMEM((d, f), F32), pltpu.VMEM((f, d), F32),
                        pltpu.VMEM((d, f), BF16), pltpu.VMEM((d, f), BF16), pltpu.VMEM((f, d), BF16),
                        pltpu.SemaphoreType.DMA((3,))],
    )
    return pl.pallas_call(
        _moe_kernel,
        grid_spec=grid_spec,
        out_shape=jax.ShapeDtypeStruct((n_slots, d), F32),
        compiler_params=_cparams(("arbitrary",)),
        name="moe",
    )(meta_flat, hs, w_gate, w_up, w_down)


def _final_kernel(pos_ref, posn_ref, y_hbm, x1_ref, route_ref, g2_ref, nw_ref, o_ref, ybuf, sems):
    i = pl.program_id(0)
    nt = pl.num_programs(0)
    tm = x1_ref.shape[0]
    slot = lax.rem(i, 2)

    def issue(p_ref, s):
        for j in range(tm):
            for k in range(2):
                pltpu.make_async_copy(y_hbm.at[pl.ds(p_ref[0, k, j], 1)],
                                      ybuf.at[s, k, pl.ds(j, 1)], sems.at[s]).start(priority=k)

    @pl.when(i == 0)
    def _():
        issue(pos_ref, 0)

    for s in range(2):
        @pl.when(jnp.logical_and(i + 1 < nt, slot == 1 - s))
        def _(s=s):
            issue(posn_ref, s)

    for k in range(2):
        pltpu.make_async_copy(y_hbm.at[pl.ds(0, tm)], ybuf.at[slot, k], sems.at[slot]).wait()

    rt = route_ref[...]
    moe = rt[:, 2:3] * ybuf[slot, 0] + rt[:, 3:4] * ybuf[slot, 1]
    x2 = x1_ref[...] + g2_ref[...] * moe
    ms = jnp.mean(x2 * x2, axis=-1, keepdims=True)
    o_ref[...] = x2 * lax.rsqrt(ms + EPS) * nw_ref[...]


def _final(y_sorted, pos, x1, route, g2, nw):
    t, d = x1.shape
    tm = POS_CHUNK
    nt = t // tm
    const = lambda i: (0, 0)
    return pl.pallas_call(
        _final_kernel,
        grid=(nt,),
        in_specs=[pl.BlockSpec((1, 2, tm), lambda i: (i, 0, 0), memory_space=pltpu.SMEM),
                  pl.BlockSpec((1, 2, tm), lambda i: (jnp.minimum(i + 1, nt - 1), 0, 0),
                               memory_space=pltpu.SMEM),
                  pl.BlockSpec(memory_space=pl.ANY),
                  pl.BlockSpec((tm, d), lambda i: (i, 0)),
                  pl.BlockSpec((tm, LANES), lambda i: (i, 0)),
                  pl.BlockSpec((1, d), const), pl.BlockSpec((1, d), const)],
        out_specs=pl.BlockSpec((tm, d), lambda i: (i, 0)),
        out_shape=jax.ShapeDtypeStruct((t, d), F32),
        scratch_shapes=[pltpu.VMEM((2, 2, tm, d), F32), pltpu.SemaphoreType.DMA((2,))],
        compiler_params=_cparams(("arbitrary",)),
        name="final",
    )(pos, pos, y_sorted, x1, route, g2, nw)


def kernel(x, c, w_ada, b_ada, norm1_w, w_in, conv_w, ssd_conv_w, ssd_conv_b, dt_bias, a_log, d_skip,
           ssd_norm_w, w_out, norm2_w, w_router_grp, b_router_grp, w_router_exp, b_router_exp,
           w_gate, w_up, w_down, final_norm_w):
    bsz, seq, d = x.shape
    assert bsz == 1 and w_ada.shape[0] == 1
    t = bsz * seq
    assert t % DISPATCH_TILE == 0 and DISPATCH_TILE % ROW_TILE == 0
    xt = x.reshape(t, d)
    gw = CONV_GROUP_W
    n_main = 5 * gw + 2 * SSD_GROUPS * SSD_STATE
    n_heads = w_in.shape[2] - n_main
    assert n_heads == SSD_HEADS

    mod = _ada(c, w_ada[0], b_ada[0])
    sh1, sc1, g1, sh2, sc2, g2 = [mod[:, k * d:(k + 1) * d] for k in range(6)]

    w_all = w_in[0].astype(BF16)
    w_dt = jnp.pad(w_all[:, n_main:], ((0, 0), (0, LANES - n_heads)))
    u, dt_raw = _inproj(xt, norm1_w, sc1, sh1, w_all, w_dt, n_main)

    pad_h = lambda v: jnp.pad(v, ((0, 0), (0, LANES - n_heads)))
    ycat = _mix(u, dt_raw, conv_w[0], ssd_conv_w[0][:, :gw], ssd_conv_w[0][:, gw:],
                ssd_conv_b[:, :gw], ssd_conv_b[:, gw:], pad_h(dt_bias), pad_h(a_log),
                jnp.repeat(d_skip, SSD_HEAD_DIM, axis=1), ssd_norm_w)

    n_r = N_EXPERTS + N_EXPERT_GROUPS
    w_rt = jnp.pad(jnp.concatenate([w_router_exp[0], w_router_grp[0]], axis=1).T,
                   ((0, ROUTER_ROWS - n_r), (0, 0)))
    wr_hi = w_rt.astype(BF16)
    wr_lo = (w_rt - wr_hi.astype(F32)).astype(BF16)
    b_r = jnp.pad(jnp.concatenate([b_router_exp, b_router_grp], axis=1),
                  ((0, 0), (0, ROUTER_ROWS - n_r))).reshape(ROUTER_ROWS, 1)
    x1, h2, rt, route = _outproj(ycat, xt, w_out[0].astype(BF16), g1, norm2_w, sc2, sh2,
                                 jnp.concatenate([wr_hi, wr_lo], axis=0), b_r)

    pos, meta = _meta(rt)
    meta_flat = meta.reshape(SUBLANES * META_LANES)
    n_blocks = (2 * t + N_EXPERTS * (MOE_ROWS - 1) + MOE_ROWS - 1) // MOE_ROWS
    assert n_blocks <= META_LANES
    hs = _dispatch(meta_flat, pos, h2, n_blocks * MOE_ROWS)
    y_sorted = _moe(meta_flat, hs, w_gate[0], w_up[0], w_down[0])
    out = _final(y_sorted, pos, x1, route, g2, final_norm_w.reshape(1, d))
    return out.reshape(bsz, seq, d)
```

```python
import functools

import jax
import jax.numpy as jnp
from jax import lax
from jax.experimental import pallas as pl
from jax.experimental.pallas import tpu as pltpu

F32 = jnp.float32
BF16 = jnp.bfloat16
U32 = jnp.uint32
I32 = jnp.int32
EPS = 1e-6

CONV_GROUP_W = 1024
SSD_HEADS = 16
SSD_HEAD_DIM = 64
SSD_GROUPS = 2
SSD_STATE = 128
N_EXPERT_GROUPS = 4
EXPERTS_PER_GROUP = 8
N_EXPERTS = 32
LANES = 128
SUBLANES = 8
VMEM_LIMIT = 56 * 1024 * 1024

SSD_CHUNK = 128
MIX_TILE = 256
CONV_CHUNK = 256
ROW_TILE = 512
DISPATCH_TILE = 1024
POS_CHUNK = 256
MOE_ROWS = 256
MOE_SUB = 2
ROUTER_ROWS = 48
META_LANES = 256
META_EXPERT, META_NVALID, META_NEXT = 0, META_LANES, 2 * META_LANES
HIGHEST = lax.Precision.HIGHEST
NT_DIMS = (((1,), (1,)), ((), ()))


def _silu(v):
    return 0.5 * v * (1.0 + jnp.tanh(0.5 * v))


def _cparams(sem):
    return pltpu.CompilerParams(dimension_semantics=sem, vmem_limit_bytes=VMEM_LIMIT)


def _ada_kernel(c_ref, w_ref, b_ref, o_ref):
    s = _silu(c_ref[...])
    o_ref[...] = jnp.sum(w_ref[...] * s, axis=0, keepdims=True) + b_ref[...]


def _ada(c, w_ada, b_ada):
    d, n = w_ada.shape
    tn = 1024
    return pl.pallas_call(
        _ada_kernel,
        grid=(n // tn,),
        in_specs=[pl.BlockSpec((d, 1), lambda j: (0, 0)),
                  pl.BlockSpec((d, tn), lambda j: (0, j)),
                  pl.BlockSpec((1, tn), lambda j: (0, j))],
        out_specs=pl.BlockSpec((1, tn), lambda j: (0, j)),
        out_shape=jax.ShapeDtypeStruct((1, n), F32),
        compiler_params=_cparams(("arbitrary",)),
        name="ada",
    )(c.reshape(d, 1), w_ada, b_ada.reshape(1, n))


def _inproj_kernel(x_ref, nw_ref, sc_ref, sh_ref, w_ref, wdt_ref, u_ref, dt_ref, h_ref, *, rows, tn):
    tm = x_ref.shape[0]
    n = w_ref.shape[1]
    scale = nw_ref[...] * (1.0 + sc_ref[...])
    for r0 in range(0, tm, rows):
        xf = x_ref[r0:r0 + rows, :]
        ms = jnp.mean(xf * xf, axis=-1, keepdims=True)
        h_ref[r0:r0 + rows, :] = (xf * lax.rsqrt(ms + EPS) * scale + sh_ref[...]).astype(BF16)
        h = h_ref[r0:r0 + rows, :]
        dt_ref[r0:r0 + rows, :] = jnp.dot(h, wdt_ref[...], preferred_element_type=F32)
        for n0 in range(0, n, tn):
            u_ref[r0:r0 + rows, n0:n0 + tn] = jnp.dot(
                h, w_ref[:, n0:n0 + tn], preferred_element_type=F32).astype(BF16)


def _inproj(x, nw, sc, sh, w_all, w_dt, n):
    t, d = x.shape
    tm = ROW_TILE
    const = lambda i: (0, 0)
    row = pl.BlockSpec((1, d), const)
    return pl.pallas_call(
        functools.partial(_inproj_kernel, rows=256, tn=512),
        grid=(t // tm,),
        in_specs=[pl.BlockSpec((tm, d), lambda i: (i, 0)), row, row, row,
                  pl.BlockSpec((d, n), const, pipeline_mode=pl.Buffered(1)),
                  pl.BlockSpec((d, LANES), const)],
        out_specs=[pl.BlockSpec((tm, n), lambda i: (i, 0)),
                   pl.BlockSpec((tm, LANES), lambda i: (i, 0))],
        out_shape=[jax.ShapeDtypeStruct((t, n), BF16), jax.ShapeDtypeStruct((t, LANES), F32)],
        scratch_shapes=[pltpu.VMEM((tm, d), BF16)],
        compiler_params=_cparams(("arbitrary",)),
        name="inproj",
    )(x, nw, sc, sh, w_all, w_dt)


def _shift_matrix(q, nshift):
    rr = lax.broadcasted_iota(I32, (q, q), 0)
    cc = lax.broadcasted_iota(I32, (q, q), 1)
    return jnp.concatenate([(rr - cc == s).astype(BF16) for s in range(1, nshift + 1)], axis=0)


def _shifted_rows(smat, x_b, x_f, tail_ref, cols, nshift):
    q = x_b.shape[0]
    sh = jnp.dot(smat[:nshift * q, :], x_b, preferred_element_type=F32)
    head = jnp.concatenate([tail_ref[:, cols], x_f[0:SUBLANES, :]], axis=0)
    outs = [jnp.concatenate([head[SUBLANES - s:2 * SUBLANES - s, :],
                             sh[(s - 1) * q + SUBLANES:s * q, :]], axis=0)
            for s in range(1, nshift + 1)]
    tail_ref[:, cols] = x_f[q - SUBLANES:, :]
    return outs


def _conv4_silu(smat, x_ref, tail_ref, w_ref, b_ref, out_ref):
    for c0 in range(0, x_ref.shape[1], CONV_CHUNK):
        cols = slice(c0, c0 + CONV_CHUNK)
        x_b = x_ref[:, cols]
        x_f = x_b.astype(F32)
        s1, s2, s3 = _shifted_rows(smat, x_b, x_f, tail_ref, cols, 3)
        conv = (w_ref[3:4, cols] * x_f + w_ref[2:3, cols] * s1 + w_ref[1:2, cols] * s2
                + w_ref[0:1, cols] * s3)
        out_ref[:, cols] = _silu(conv + b_ref[:, cols])


def _expand_heads(v, left):
    parts = [jnp.where(left, v[:, 2 * k:2 * k + 1], v[:, 2 * k + 1:2 * k + 2])
             for k in range(SSD_HEADS // 2)]
    return jnp.concatenate(parts, axis=1)


def _mix_kernel(ub_ref, uc_ref, uv_ref, uz_ref, uxs_ref, ubc_ref, dt_ref,
                cw_ref, wxs_ref, wbc_ref, bxs_ref, bbc_ref, dtb_ref, alog_ref, dskip_ref,
                nw_ref, o_ref,
                tail_c, tail_v, tail_xs, tail_bc, state_ref, dec_ref, xs_ref, bc_ref):
    i = pl.program_id(0)
    tq = ub_ref.shape[0]
    q = SSD_CHUNK
    gw = SSD_HEADS // SSD_GROUPS * SSD_HEAD_DIM
    hpg = SSD_HEADS // SSD_GROUPS

    @pl.when(i == 0)
    def _():
        for tail in (tail_c, tail_v, tail_xs, tail_bc):
            tail[...] = jnp.zeros_like(tail)
        state_ref[...] = jnp.zeros_like(state_ref)

    dtv = dt_ref[...] + dtb_ref[...]
    dt_all = jnp.maximum(dtv, 0.0) + jnp.log1p(jnp.exp(-jnp.abs(dtv)))
    a_all = dt_all * (-jnp.exp(alog_ref[...]))
    rr = lax.broadcasted_iota(I32, (q, q), 0)
    cc = lax.broadcasted_iota(I32, (q, q), 1)
    causal = rr >= cc
    tri = causal.astype(F32)
    left = lax.broadcasted_iota(I32, (q, LANES), 1) < SSD_HEAD_DIM
    decays = []
    for ci, r0 in enumerate(range(0, tq, q)):
        dt = dt_all[r0:r0 + q, :]
        a_cs = jnp.dot(tri, a_all[r0:r0 + q, :], precision=HIGHEST,
                       preferred_element_type=F32)
        a_cs_t = a_cs.T
        dt_t = dt.T
        last = a_cs[q - 1:q, :]
        sd_x = _expand_heads(jnp.exp(a_cs), left)
        w_x = _expand_heads(dt * jnp.exp(last - a_cs), left)
        cd_x = _expand_heads(jnp.exp(last), left[0:1, :])
        decays.append((sd_x, w_x, cd_x))
        for h in range(SSD_HEADS):
            decay = jnp.exp(a_cs[:, h:h + 1] - a_cs_t[h:h + 1, :])
            dec_ref[ci * SSD_HEADS + h] = jnp.where(causal, decay * dt_t[h:h + 1, :], 0.0)

    smat = _shift_matrix(tq, 3)

    for c0 in range(0, CONV_GROUP_W, CONV_CHUNK):
        cols = slice(c0, c0 + CONV_CHUNK)
        c_b, v_b = uc_ref[:, cols], uv_ref[:, cols]
        c_f, v_f = c_b.astype(F32), v_b.astype(F32)
        c1, c2 = _shifted_rows(smat, c_b, c_f, tail_c, cols, 2)
        v1, v2 = _shifted_rows(smat, v_b, v_f, tail_v, cols, 2)
        conv = (cw_ref[2:3, cols] * (c_f * v_f) + cw_ref[1:2, cols] * (c1 * v1)
                + cw_ref[0:1, cols] * (c2 * v2))
        o_ref[:, cols] = (ub_ref[:, cols].astype(F32) * conv).astype(BF16)

    _conv4_silu(smat, uxs_ref, tail_xs, wxs_ref, bxs_ref, xs_ref)
    _conv4_silu(smat, ubc_ref, tail_bc, wbc_ref, bbc_ref, bc_ref)
    xs_all = xs_ref[...]
    bc_all = bc_ref[...]
    gs = SSD_GROUPS * SSD_STATE

    y_chunks = []
    for ci, r0 in enumerate(range(0, tq, q)):
        xs = xs_all[r0:r0 + q, :]
        bm = bc_all[r0:r0 + q, :gs]
        cm = bc_all[r0:r0 + q, gs:]
        sd_x, w_x, cd_x = decays[ci]
        xs_b = xs.astype(BF16)
        xw_b = (xs * w_x).astype(BF16)

        y_parts = []
        for g in range(SSD_GROUPS):
            cm_g = cm[:, g * SSD_STATE:(g + 1) * SSD_STATE].astype(BF16)
            bm_g = bm[:, g * SSD_STATE:(g + 1) * SSD_STATE]
            cb = lax.dot_general(cm_g, bm_g.astype(BF16), NT_DIMS, preferred_element_type=F32)
            s_prev = state_ref[g]
            y_off = jnp.dot(cm_g, s_prev.astype(BF16), preferred_element_type=F32)
            y_off = y_off * sd_x[:, g * gw:(g + 1) * gw]
            for pr in range(hpg // 2):
                h0 = ci * SSD_HEADS + g * hpg + pr * 2
                lhs = jnp.concatenate([(cb * dec_ref[h0]).astype(BF16),
                                       (cb * dec_ref[h0 + 1]).astype(BF16)], axis=1)
                c0 = g * gw + pr * LANES
                xp = xs_b[:, c0:c0 + LANES]
                zero = jnp.zeros_like(xp)
                rhs = jnp.concatenate([jnp.where(left, xp, zero), jnp.where(left, zero, xp)], axis=0)
                y_parts.append(jnp.dot(lhs, rhs, preferred_element_type=F32)
                               + y_off[:, pr * LANES:(pr + 1) * LANES])
            upd = jnp.dot(bm_g.T.astype(BF16), xw_b[:, g * gw:(g + 1) * gw],
                          preferred_element_type=F32)
            state_ref[g] = s_prev * cd_x[:, g * gw:(g + 1) * gw] + upd
        y_chunks.append(jnp.concatenate(y_parts, axis=1))
    y = jnp.concatenate(y_chunks, axis=0) + xs_all * dskip_ref[...]

    yg = y * _silu(uz_ref[...].astype(F32))
    outs = []
    for g in range(SSD_GROUPS):
        v = yg[:, g * gw:(g + 1) * gw]
        ms = jnp.mean(v * v, axis=-1, keepdims=True)
        outs.append(v * lax.rsqrt(ms + EPS))
    y_ssd = jnp.concatenate(outs, axis=1) * nw_ref[...]
    o_ref[:, CONV_GROUP_W:] = y_ssd.astype(BF16)


def _mix(u, dt_raw, conv_w, wxs, wbc, bxs, bbc, dtb, alog, dskip_x, nw):
    t = u.shape[0]
    tq = MIX_TILE
    gw = CONV_GROUP_W
    bcw = 2 * SSD_GROUPS * SSD_STATE
    const = lambda i: (0, 0)
    ublk = lambda k: pl.BlockSpec((tq, gw), lambda i, k=k: (i, k))
    full = lambda arr: pl.BlockSpec(arr.shape, const)
    return pl.pallas_call(
        _mix_kernel,
        grid=(t // tq,),
        in_specs=[ublk(0), ublk(1), ublk(2), ublk(3), ublk(4),
                  pl.BlockSpec((tq, bcw), lambda i: (i, 5 * gw // bcw)),
                  pl.BlockSpec((tq, LANES), lambda i: (i, 0)),
                  full(conv_w), full(wxs), full(wbc), full(bxs), full(bbc), full(dtb), full(alog),
                  full(dskip_x), full(nw)],
        out_specs=pl.BlockSpec((tq, 2 * gw), lambda i: (i, 0)),
        out_shape=jax.ShapeDtypeStruct((t, 2 * gw), BF16),
        scratch_shapes=[pltpu.VMEM((SUBLANES, gw), F32), pltpu.VMEM((SUBLANES, gw), F32),
                        pltpu.VMEM((SUBLANES, gw), F32), pltpu.VMEM((SUBLANES, bcw), F32),
                        pltpu.VMEM((SSD_GROUPS, SSD_STATE, gw // SSD_GROUPS), F32),
                        pltpu.VMEM((tq // SSD_CHUNK * SSD_HEADS, SSD_CHUNK, SSD_CHUNK), F32),
                        pltpu.VMEM((tq, gw), F32), pltpu.VMEM((tq, bcw), F32)],
        compiler_params=_cparams(("arbitrary",)),
        name="mix",
    )(u, u, u, u, u, u, dt_raw, conv_w, wxs, wbc, bxs, bbc, dtb, alog, dskip_x, nw)


def _outproj_kernel(y_ref, x_ref, w_ref, g1_ref, nw_ref, sc_ref, sh_ref, wrh_ref, br_ref,
                    x1_ref, h_ref, rt_ref, route_ref, acc_ref, hb_ref, hl_ref, *, rows):
    tm, d = x_ref.shape
    scale = nw_ref[...] * (1.0 + sc_ref[...])
    acc_ref[...] = jnp.dot(y_ref[...], w_ref[...], preferred_element_type=F32)
    for r0 in range(0, tm, rows):
        x1 = x_ref[r0:r0 + rows, :] + g1_ref[...] * acc_ref[r0:r0 + rows, :]
        x1_ref[r0:r0 + rows, :] = x1
        ms = jnp.mean(x1 * x1, axis=-1, keepdims=True)
        h = x1 * lax.rsqrt(ms + EPS) * scale + sh_ref[...]
        h_ref[r0:r0 + rows, :] = h
        hb = h.astype(BF16)
        hb_ref[r0:r0 + rows, :] = hb
        hl_ref[r0:r0 + rows, :] = (h - hb.astype(F32)).astype(BF16)

    both = lax.dot_general(wrh_ref[...], hb_ref[...], NT_DIMS, preferred_element_type=F32)
    lt = (both[:ROUTER_ROWS, :] + both[ROUTER_ROWS:, :]
          + lax.dot_general(wrh_ref[:ROUTER_ROWS, :], hl_ref[...], NT_DIMS, preferred_element_type=F32)
          + br_ref[...])
    sub = lax.broadcasted_iota(I32, (ROUTER_ROWS, tm), 0).astype(F32)
    ninf = jnp.float32(-jnp.inf)
    big = jnp.float32(1e9)
    is_g = (sub >= N_EXPERTS) & (sub < N_EXPERTS + N_EXPERT_GROUPS)
    gl = jnp.where(is_g, lt, ninf)
    gmax = jnp.max(gl, axis=0, keepdims=True)
    grp_p = 1.0 / jnp.sum(jnp.exp(gl - gmax), axis=0, keepdims=True)
    gidx = jnp.min(jnp.where(gl == gmax, sub - N_EXPERTS, big), axis=0, keepdims=True)
    lo = gidx * EXPERTS_PER_GROUP
    el = jnp.where((sub >= lo) & (sub < lo + EXPERTS_PER_GROUP), lt, ninf)
    m1 = jnp.max(el, axis=0, keepdims=True)
    i1 = jnp.min(jnp.where(el == m1, sub, big), axis=0, keepdims=True)
    el2 = jnp.where(sub == i1, ninf, el)
    m2 = jnp.max(el2, axis=0, keepdims=True)
    i2 = jnp.min(jnp.where(el2 == m2, sub, big), axis=0, keepdims=True)
    e = jnp.exp(m2 - m1)
    w1 = grp_p / (1.0 + e)
    w2 = grp_p * e / (1.0 + e)
    rt = jnp.where(sub == 0, i1, jnp.where(sub == 1, i2,
                   jnp.where(sub == 2, w1, jnp.where(sub == 3, w2, 0.0))))
    for c, r0 in enumerate(range(0, tm, rows)):
        rt_ref[c] = rt[:SUBLANES, r0:r0 + rows]
    route_ref[...] = jnp.concatenate([rt, jnp.zeros((LANES - ROUTER_ROWS, tm), F32)], axis=0).T


def _outproj(ycat, x, w_out, g1, nw, sc, sh, wr_hilo, b_r):
    t, d = x.shape
    tm = ROW_TILE
    rows = POS_CHUNK
    const = lambda i: (0, 0)
    row = pl.BlockSpec((1, d), const)
    tile = pl.BlockSpec((tm, d), lambda i: (i, 0))
    return pl.pallas_call(
        functools.partial(_outproj_kernel, rows=rows),
        grid=(t // tm,),
        in_specs=[tile, tile,
                  pl.BlockSpec((d, d), const, pipeline_mode=pl.Buffered(1)), row, row, row, row,
                  pl.BlockSpec((2 * ROUTER_ROWS, d), const),
                  pl.BlockSpec((ROUTER_ROWS, 1), const)],
        out_specs=[tile, tile,
                   pl.BlockSpec((tm // rows, SUBLANES, rows), lambda i: (i, 0, 0)),
                   pl.BlockSpec((tm, LANES), lambda i: (i, 0))],
        out_shape=[jax.ShapeDtypeStruct((t, d), F32), jax.ShapeDtypeStruct((t, d), F32),
                   jax.ShapeDtypeStruct((t // rows, SUBLANES, rows), F32),
                   jax.ShapeDtypeStruct((t, LANES), F32)],
        scratch_shapes=[pltpu.VMEM((tm, d), F32), pltpu.VMEM((tm, d), BF16), pltpu.VMEM((tm, d), BF16)],
        compiler_params=_cparams(("arbitrary",)),
        name="outproj",
    )(ycat, x, w_out, g1, nw, sc, sh, wr_hilo, b_r)


def _meta_kernel(rt_ref, pos_ref, meta_ref):
    nchunk, _, c = rt_ref.shape
    r = float(MOE_ROWS)
    sub = lax.broadcasted_iota(I32, (LANES, c), 0).astype(F32)

    def onehots(k):
        rt = rt_ref[k]
        return (sub == rt[0:1, :]).astype(F32), (sub == rt[1:2, :]).astype(F32)

    def count(k, cnt):
        oh1, oh2 = onehots(k)
        return cnt + jnp.sum(oh1 + oh2, axis=1, keepdims=True)

    cnt = lax.fori_loop(0, nchunk, count, jnp.zeros((LANES, 1), F32))
    padded = jnp.floor((cnt + (r - 1.0)) * (1.0 / r)) * r
    er = lax.broadcasted_iota(I32, (LANES, LANES), 0)
    ec = lax.broadcasted_iota(I32, (LANES, LANES), 1)
    strict_lower = (ec < er).astype(F32)
    pstart = jnp.dot(strict_lower, jnp.broadcast_to(padded, (LANES, LANES)), precision=HIGHEST,
                     preferred_element_type=F32)[:, 0:1]
    tr = lax.broadcasted_iota(I32, (c, c), 0)
    tc = lax.broadcasted_iota(I32, (c, c), 1)
    before = (tr < tc).astype(BF16)

    def place(k, run):
        oh1, oh2 = onehots(k)
        both = oh1 + oh2
        prior = jnp.dot(both.astype(BF16), before, preferred_element_type=F32)
        slot = prior + (pstart + run)
        p1 = jnp.sum(oh1 * slot, axis=0, keepdims=True)
        p2 = jnp.sum(oh2 * slot, axis=0, keepdims=True)
        pos_ref[k] = jnp.concatenate([p1, p2], axis=0).astype(I32)
        return run + jnp.sum(both, axis=1, keepdims=True)

    lax.fori_loop(0, nchunk, place, jnp.zeros((LANES, 1), F32))

    nbl = META_LANES
    e_sub = lax.broadcasted_iota(I32, (LANES, nbl), 0).astype(F32)
    blk0 = lax.broadcasted_iota(I32, (LANES, nbl), 1).astype(F32) * r
    pend = pstart + padded
    bexp = jnp.sum(jnp.where((e_sub < N_EXPERTS) & (pend <= blk0), 1.0, 0.0), axis=0, keepdims=True)
    bexp = jnp.minimum(bexp, N_EXPERTS - 1.0)
    cend = jnp.sum(jnp.where(e_sub == bexp, pstart + cnt, 0.0), axis=0, keepdims=True)
    nvalid = jnp.clip(cend - blk0[0:1, :], 0.0, r)
    total = jnp.sum(padded, axis=0, keepdims=True)
    used = blk0[0:1, :] < total
    nvalid = jnp.where(used, nvalid, 0.0)
    pend_row = jnp.sum(jnp.where(er == ec, pend, 0.0), axis=0, keepdims=True)
    b_sub = lax.broadcasted_iota(I32, (nbl, LANES), 0).astype(F32) * r
    e_lane = lax.broadcasted_iota(I32, (nbl, LANES), 1)
    bexp_col = jnp.sum(jnp.where((e_lane < N_EXPERTS) & (pend_row <= b_sub), 1.0, 0.0),
                       axis=1, keepdims=True)
    bexp_col = jnp.minimum(bexp_col, N_EXPERTS - 1.0)
    seg_end = jnp.sum(jnp.where(e_sub == bexp, pend, 0.0), axis=0, keepdims=True)
    bp_sub = lax.broadcasted_iota(I32, (nbl, nbl), 0).astype(F32) * r
    nxt = jnp.sum(jnp.where(bp_sub == seg_end, bexp_col, 0.0), axis=0, keepdims=True)
    nxt = jnp.where(used & (seg_end < total), nxt, -1.0)
    rowsel = lax.broadcasted_iota(I32, (SUBLANES, nbl), 0)
    meta = jnp.where(rowsel == 0, bexp, jnp.where(rowsel == 1, nvalid, jnp.where(rowsel == 2, nxt, 0.0)))
    meta_ref[...] = meta.astype(I32)


def _meta(rt):
    nchunk, _, c = rt.shape
    return pl.pallas_call(
        _meta_kernel,
        out_shape=[jax.ShapeDtypeStruct((nchunk, 2, c), I32),
                   jax.ShapeDtypeStruct((SUBLANES, META_LANES), I32)],
        compiler_params=pltpu.CompilerParams(vmem_limit_bytes=VMEM_LIMIT),
        name="meta",
    )(rt)


def _dispatch_kernel(meta_ref, pos_ref, h_ref, hs_hbm, zbuf, sem, zsem):
    i = pl.program_id(0)
    tm = h_ref.shape[0]
    nck, _, c = pos_ref.shape
    r = MOE_ROWS

    @pl.when(i == 0)
    def _():
        zbuf[...] = jnp.zeros_like(zbuf)

        def zero_copy(b):
            return pltpu.make_async_copy(zbuf, hs_hbm.at[pl.ds(pl.multiple_of(b * r, r), r)], zsem)

        def zstart(b, carry):
            @pl.when(meta_ref[META_NVALID + b] < r)
            def _():
                zero_copy(b).start()
            return carry

        def zwait(b, carry):
            @pl.when(meta_ref[META_NVALID + b] < r)
            def _():
                zero_copy(b).wait()
            return carry

        lax.fori_loop(0, hs_hbm.shape[0] // r, zstart, 0)
        lax.fori_loop(0, hs_hbm.shape[0] // r, zwait, 0)

    for ck in range(nck):
        for j in range(c):
            row = ck * c + j
            for k in range(2):
                pltpu.make_async_copy(h_ref.at[pl.ds(row, 1)],
                                      hs_hbm.at[pl.ds(pos_ref[ck, k, j], 1)], sem).start(priority=k)
    for k in range(2):
        pltpu.make_async_copy(h_ref, hs_hbm.at[pl.ds(0, tm)], sem).wait()


def _dispatch(meta_flat, pos, h, n_slots):
    t, d = h.shape
    tm = DISPATCH_TILE
    nck = tm // POS_CHUNK
    grid_spec = pltpu.PrefetchScalarGridSpec(
        num_scalar_prefetch=1,
        grid=(t // tm,),
        in_specs=[pl.BlockSpec((nck, 2, POS_CHUNK), lambda i, m: (i, 0, 0), memory_space=pltpu.SMEM),
                  pl.BlockSpec((tm, d), lambda i, m: (i, 0))],
        out_specs=pl.BlockSpec(memory_space=pl.ANY),
        scratch_shapes=[pltpu.VMEM((MOE_ROWS, d), F32), pltpu.SemaphoreType.DMA(()),
                        pltpu.SemaphoreType.DMA(())],
    )
    return pl.pallas_call(
        _dispatch_kernel,
        grid_spec=grid_spec,
        out_shape=jax.ShapeDtypeStruct((n_slots, d), F32),
        compiler_params=_cparams(("arbitrary",)),
        name="dispatch",
    )(meta_flat, pos, h)


def _moe_kernel(meta_ref, x_ref, wg_hbm, wu_hbm, wd_hbm, y_ref, *scratch):
    for s in range(MOE_SUB):
        rows = pl.ds(s * MOE_ROWS, MOE_ROWS)
        _moe_block(meta_ref, pl.program_id(0) * MOE_SUB + s, x_ref.at[rows], wg_hbm, wu_hbm, wd_hbm,
                   y_ref.at[rows], *scratch)


def _moe_block(meta_ref, b, x_ref, wg_hbm, wu_hbm, wd_hbm, y_ref, stg_g, stg_u, stg_d, wgb, wub, wdb, sems):
    e = meta_ref[META_EXPERT + b]
    nxt = meta_ref[META_NEXT + b]
    used = meta_ref[META_NVALID + b] > 0
    prev = meta_ref[META_EXPERT + jnp.maximum(b - 1, 0)]
    changed = jnp.logical_and(used, jnp.logical_or(b == 0, e != prev))
    d, f = wgb.shape
    kc = 512

    def copies(ex):
        return [pltpu.make_async_copy(w.at[ex], stg, sems.at[k])
                for k, (w, stg) in enumerate(((wg_hbm, stg_g), (wu_hbm, stg_u), (wd_hbm, stg_d)))]

    @pl.when(b == 0)
    def _():
        for cp in copies(e):
            cp.start()

    def convert_dot(lhs, stg, wb, n):
        acc = None
        for k0 in range(0, n, kc):
            w = stg[k0:k0 + kc, :].astype(BF16)
            wb[k0:k0 + kc, :] = w
            part = jnp.dot(lhs[:, k0:k0 + kc], w, preferred_element_type=F32)
            acc = part if acc is None else acc + part
        return acc

    @pl.when(changed)
    def _():
        cps = copies(e)
        x = x_ref[...].astype(BF16)
        cps[0].wait()
        gate = convert_dot(x, stg_g, wgb, d)
        cps[1].wait()
        up = convert_dot(x, stg_u, wub, d)
        hid = (_silu(gate) * up).astype(BF16)
        cps[2].wait()
        y_ref[...] = convert_dot(hid, stg_d, wdb, f)

        @pl.when(nxt >= 0)
        def _():
            for cp in copies(nxt):
                cp.start(priority=1)

    @pl.when(jnp.logical_and(used, jnp.logical_not(changed)))
    def _():
        x = x_ref[...].astype(BF16)
        gate = jnp.dot(x, wgb[...], preferred_element_type=F32)
        up = jnp.dot(x, wub[...], preferred_element_type=F32)
        hid = (_silu(gate) * up).astype(BF16)
        y_ref[...] = jnp.dot(hid, wdb[...], preferred_element_type=F32)

    @pl.when(jnp.logical_not(used))
    def _():
        y_ref[...] = jnp.zeros_like(y_ref)


def _moe(meta_flat, hs, w_gate, w_up, w_down):
    n_slots, d = hs.shape
    f = w_gate.shape[2]
    hbm = pl.BlockSpec(memory_space=pl.ANY)
    step_rows = MOE_SUB * MOE_ROWS
    grid_spec = pltpu.PrefetchScalarGridSpec(
        num_scalar_prefetch=1,
        grid=(n_slots // step_rows,),
        in_specs=[pl.BlockSpec((step_rows, d),
                               lambda g, m: (jnp.where(m[META_NVALID + g * MOE_SUB] > 0, g, 0), 0)),
                  hbm, hbm, hbm],
        out_specs=pl.BlockSpec((step_rows, d), lambda g, m: (g, 0)),
        scratch_shapes=[pltpu.VMEM((d, f), F32), pltpu.VMEM((d, f), F32), pltpu.VMEM((f, d), F32),
                        pltpu.VMEM((d, f), BF16), pltpu.VMEM((d, f), BF16), pltpu.VMEM((f, d), BF16),
                        pltpu.SemaphoreType.DMA((3,))],
    )
    return pl.pallas_call(
        _moe_kernel,
        grid_spec=grid_spec,
        out_shape=jax.ShapeDtypeStruct((n_slots, d), F32),
        compiler_params=_cparams(("arbitrary",)),
        name="moe",
    )(meta_flat, hs, w_gate, w_up, w_down)


def _final_kernel(pos_ref, posn_ref, y_hbm, x1_ref, route_ref, g2_ref, nw_ref, o_ref, ybuf, sems):
    i = pl.program_id(0)
    nt = pl.num_programs(0)
    tm = x1_ref.shape[0]
    slot = lax.rem(i, 2)

    def issue(p_ref, s):
        c = p_ref.shape[2]
        for j in range(tm):
            for k in range(2):
                pltpu.make_async_copy(y_hbm.at[pl.ds(p_ref[j // c, k, j % c], 1)],
                                      ybuf.at[s, k, pl.ds(j, 1)], sems.at[s]).start(priority=k)

    @pl.when(i == 0)
    def _():
        issue(pos_ref, 0)

    for s in range(2):
        @pl.when(jnp.logical_and(i + 1 < nt, slot == 1 - s))
        def _(s=s):
            issue(posn_ref, s)

    for k in range(2):
        pltpu.make_async_copy(y_hbm.at[pl.ds(0, tm)], ybuf.at[slot, k], sems.at[slot]).wait()

    rt = route_ref[...]
    moe = rt[:, 2:3] * ybuf[slot, 0] + rt[:, 3:4] * ybuf[slot, 1]
    x2 = x1_ref[...] + g2_ref[...] * moe
    ms = jnp.mean(x2 * x2, axis=-1, keepdims=True)
    o_ref[...] = x2 * lax.rsqrt(ms + EPS) * nw_ref[...]


def _final(y_sorted, pos, x1, route, g2, nw):
    t, d = x1.shape
    tm = ROW_TILE
    nt = t // tm
    nck = tm // POS_CHUNK
    const = lambda i: (0, 0)
    return pl.pallas_call(
        _final_kernel,
        grid=(nt,),
        in_specs=[pl.BlockSpec((nck, 2, POS_CHUNK), lambda i: (i, 0, 0), memory_space=pltpu.SMEM),
                  pl.BlockSpec((nck, 2, POS_CHUNK), lambda i: (jnp.minimum(i + 1, nt - 1), 0, 0),
                               memory_space=pltpu.SMEM),
                  pl.BlockSpec(memory_space=pl.ANY),
                  pl.BlockSpec((tm, d), lambda i: (i, 0)),
                  pl.BlockSpec((tm, LANES), lambda i: (i, 0)),
                  pl.BlockSpec((1, d), const), pl.BlockSpec((1, d), const)],
        out_specs=pl.BlockSpec((tm, d), lambda i: (i, 0)),
        out_shape=jax.ShapeDtypeStruct((t, d), F32),
        scratch_shapes=[pltpu.VMEM((2, 2, tm, d), F32), pltpu.SemaphoreType.DMA((2,))],
        compiler_params=_cparams(("arbitrary",)),
        name="final",
    )(pos, pos, y_sorted, x1, route, g2, nw)


def kernel(x, c, w_ada, b_ada, norm1_w, w_in, conv_w, ssd_conv_w, ssd_conv_b, dt_bias, a_log, d_skip,
           ssd_norm_w, w_out, norm2_w, w_router_grp, b_router_grp, w_router_exp, b_router_exp,
           w_gate, w_up, w_down, final_norm_w):
    bsz, seq, d = x.shape
    assert bsz == 1 and w_ada.shape[0] == 1
    t = bsz * seq
    assert t % DISPATCH_TILE == 0 and DISPATCH_TILE % ROW_TILE == 0
    xt = x.reshape(t, d)
    gw = CONV_GROUP_W
    n_main = 5 * gw + 2 * SSD_GROUPS * SSD_STATE
    n_heads = w_in.shape[2] - n_main
    assert n_heads == SSD_HEADS

    mod = _ada(c, w_ada[0], b_ada[0])
    sh1, sc1, g1, sh2, sc2, g2 = [mod[:, k * d:(k + 1) * d] for k in range(6)]

    w_all = w_in[0].astype(BF16)
    w_dt = jnp.pad(w_all[:, n_main:], ((0, 0), (0, LANES - n_heads)))
    u, dt_raw = _inproj(xt, norm1_w, sc1, sh1, w_all, w_dt, n_main)

    pad_h = lambda v: jnp.pad(v, ((0, 0), (0, LANES - n_heads)))
    ycat = _mix(u, dt_raw, conv_w[0], ssd_conv_w[0][:, :gw], ssd_conv_w[0][:, gw:],
                ssd_conv_b[:, :gw], ssd_conv_b[:, gw:], pad_h(dt_bias), pad_h(a_log),
                jnp.repeat(d_skip, SSD_HEAD_DIM, axis=1), ssd_norm_w)

    n_r = N_EXPERTS + N_EXPERT_GROUPS
    w_rt = jnp.pad(jnp.concatenate([w_router_exp[0], w_router_grp[0]], axis=1).T,
                   ((0, ROUTER_ROWS - n_r), (0, 0)))
    wr_hi = w_rt.astype(BF16)
    wr_lo = (w_rt - wr_hi.astype(F32)).astype(BF16)
    b_r = jnp.pad(jnp.concatenate([b_router_exp, b_router_grp], axis=1),
                  ((0, 0), (0, ROUTER_ROWS - n_r))).reshape(ROUTER_ROWS, 1)
    x1, h2, rt, route = _outproj(ycat, xt, w_out[0].astype(BF16), g1, norm2_w, sc2, sh2,
                                 jnp.concatenate([wr_hi, wr_lo], axis=0), b_r)

    pos, meta = _meta(rt)
    meta_flat = meta.reshape(SUBLANES * META_LANES)
    n_blocks = (2 * t + N_EXPERTS * (MOE_ROWS - 1) + MOE_ROWS - 1) // MOE_ROWS
    n_blocks = (n_blocks + MOE_SUB - 1) // MOE_SUB * MOE_SUB
    assert n_blocks <= META_LANES
    hs = _dispatch(meta_flat, pos, h2, n_blocks * MOE_ROWS)
    y_sorted = _moe(meta_flat, hs, w_gate[0], w_up[0], w_down[0])
    out = _final(y_sorted, pos, x1, route, g2, final_norm_w.reshape(1, d))
    return out.reshape(bsz, seq, d)
```

```python
import functools

import jax
import jax.numpy as jnp
from jax import lax
from jax.experimental import pallas as pl
from jax.experimental.pallas import tpu as pltpu

F32 = jnp.float32
BF16 = jnp.bfloat16
U32 = jnp.uint32
I32 = jnp.int32
EPS = 1e-6

CONV_GROUP_W = 1024
SSD_HEADS = 16
SSD_HEAD_DIM = 64
SSD_GROUPS = 2
SSD_STATE = 128
N_EXPERT_GROUPS = 4
EXPERTS_PER_GROUP = 8
N_EXPERTS = 32
LANES = 128
SUBLANES = 8
VMEM_LIMIT = 56 * 1024 * 1024

SSD_CHUNK = 128
MIX_TILE = 256
CONV_CHUNK = 256
ROW_TILE = 512
DISPATCH_TILE = 1024
POS_CHUNK = 256
MOE_ROWS = 256
ROUTER_ROWS = 48
META_LANES = 256
META_EXPERT, META_NVALID, META_NEXT = 0, META_LANES, 2 * META_LANES
HIGHEST = lax.Precision.HIGHEST
NT_DIMS = (((1,), (1,)), ((), ()))


def _silu(v):
    return 0.5 * v * (1.0 + jnp.tanh(0.5 * v))


def _cparams(sem):
    return pltpu.CompilerParams(dimension_semantics=sem, vmem_limit_bytes=VMEM_LIMIT)


def _ada_kernel(c_ref, w_ref, b_ref, o_ref):
    s = _silu(c_ref[...])
    o_ref[...] = jnp.sum(w_ref[...] * s, axis=0, keepdims=True) + b_ref[...]


def _ada(c, w_ada, b_ada):
    d, n = w_ada.shape
    tn = 1024
    return pl.pallas_call(
        _ada_kernel,
        grid=(n // tn,),
        in_specs=[pl.BlockSpec((d, 1), lambda j: (0, 0)),
                  pl.BlockSpec((d, tn), lambda j: (0, j)),
                  pl.BlockSpec((1, tn), lambda j: (0, j))],
        out_specs=pl.BlockSpec((1, tn), lambda j: (0, j)),
        out_shape=jax.ShapeDtypeStruct((1, n), F32),
        compiler_params=_cparams(("arbitrary",)),
        name="ada",
    )(c.reshape(d, 1), w_ada, b_ada.reshape(1, n))


def _inproj_kernel(x_ref, nw_ref, sc_ref, sh_ref, w_ref, wdt_ref, u_ref, dt_ref, h_ref, *, rows, tn):
    tm = x_ref.shape[0]
    n = w_ref.shape[1]
    scale = nw_ref[...] * (1.0 + sc_ref[...])
    for r0 in range(0, tm, rows):
        xf = x_ref[r0:r0 + rows, :]
        ms = jnp.mean(xf * xf, axis=-1, keepdims=True)
        h_ref[r0:r0 + rows, :] = (xf * lax.rsqrt(ms + EPS) * scale + sh_ref[...]).astype(BF16)
        h = h_ref[r0:r0 + rows, :]
        dt_ref[r0:r0 + rows, :] = jnp.dot(h, wdt_ref[...], preferred_element_type=F32)
        for n0 in range(0, n, tn):
            u_ref[r0:r0 + rows, n0:n0 + tn] = jnp.dot(
                h, w_ref[:, n0:n0 + tn], preferred_element_type=F32).astype(BF16)


def _inproj(x, nw, sc, sh, w_all, w_dt, n):
    t, d = x.shape
    tm = ROW_TILE
    const = lambda i: (0, 0)
    row = pl.BlockSpec((1, d), const)
    return pl.pallas_call(
        functools.partial(_inproj_kernel, rows=256, tn=512),
        grid=(t // tm,),
        in_specs=[pl.BlockSpec((tm, d), lambda i: (i, 0)), row, row, row,
                  pl.BlockSpec((d, n), const, pipeline_mode=pl.Buffered(1)),
                  pl.BlockSpec((d, LANES), const)],
        out_specs=[pl.BlockSpec((tm, n), lambda i: (i, 0)),
                   pl.BlockSpec((tm, LANES), lambda i: (i, 0))],
        out_shape=[jax.ShapeDtypeStruct((t, n), BF16), jax.ShapeDtypeStruct((t, LANES), F32)],
        scratch_shapes=[pltpu.VMEM((tm, d), BF16)],
        compiler_params=_cparams(("arbitrary",)),
        name="inproj",
    )(x, nw, sc, sh, w_all, w_dt)


def _shift_matrix(q, nshift):
    rr = lax.broadcasted_iota(I32, (q, q), 0)
    cc = lax.broadcasted_iota(I32, (q, q), 1)
    return jnp.concatenate([(rr - cc == s).astype(BF16) for s in range(1, nshift + 1)], axis=0)


def _shifted_rows(smat, x_b, x_f, tail_ref, cols, nshift):
    q = x_b.shape[0]
    sh = jnp.dot(smat[:nshift * q, :], x_b, preferred_element_type=F32)
    head = jnp.concatenate([tail_ref[:, cols], x_f[0:SUBLANES, :]], axis=0)
    outs = [jnp.concatenate([head[SUBLANES - s:2 * SUBLANES - s, :],
                             sh[(s - 1) * q + SUBLANES:s * q, :]], axis=0)
            for s in range(1, nshift + 1)]
    tail_ref[:, cols] = x_f[q - SUBLANES:, :]
    return outs


def _conv4_silu(smat, x_ref, tail_ref, w_ref, b_ref, out_ref):
    for c0 in range(0, x_ref.shape[1], CONV_CHUNK):
        cols = slice(c0, c0 + CONV_CHUNK)
        x_b = x_ref[:, cols]
        x_f = x_b.astype(F32)
        s1, s2, s3 = _shifted_rows(smat, x_b, x_f, tail_ref, cols, 3)
        conv = (w_ref[3:4, cols] * x_f + w_ref[2:3, cols] * s1 + w_ref[1:2, cols] * s2
                + w_ref[0:1, cols] * s3)
        out_ref[:, cols] = _silu(conv + b_ref[:, cols])


def _expand_heads(v, left):
    parts = [jnp.where(left, v[:, 2 * k:2 * k + 1], v[:, 2 * k + 1:2 * k + 2])
             for k in range(SSD_HEADS // 2)]
    return jnp.concatenate(parts, axis=1)


def _mix_kernel(ub_ref, uc_ref, uv_ref, uz_ref, uxs_ref, ubc_ref, dt_ref,
                cw_ref, wxs_ref, wbc_ref, bxs_ref, bbc_ref, dtb_ref, alog_ref, dskip_ref,
                nw_ref, o_ref,
                tail_c, tail_v, tail_xs, tail_bc, state_ref, dec_ref, xs_ref, bc_ref):
    i = pl.program_id(0)
    tq = ub_ref.shape[0]
    q = SSD_CHUNK
    gw = SSD_HEADS // SSD_GROUPS * SSD_HEAD_DIM
    hpg = SSD_HEADS // SSD_GROUPS

    @pl.when(i == 0)
    def _():
        for tail in (tail_c, tail_v, tail_xs, tail_bc):
            tail[...] = jnp.zeros_like(tail)
        state_ref[...] = jnp.zeros_like(state_ref)

    dtv = dt_ref[...] + dtb_ref[...]
    dt_all = jnp.maximum(dtv, 0.0) + jnp.log1p(jnp.exp(-jnp.abs(dtv)))
    a_all = dt_all * (-jnp.exp(alog_ref[...]))
    rr = lax.broadcasted_iota(I32, (q, q), 0)
    cc = lax.broadcasted_iota(I32, (q, q), 1)
    causal = rr >= cc
    tri = causal.astype(F32)
    left = lax.broadcasted_iota(I32, (q, LANES), 1) < SSD_HEAD_DIM
    decays = []
    for ci, r0 in enumerate(range(0, tq, q)):
        dt = dt_all[r0:r0 + q, :]
        a_cs = jnp.dot(tri, a_all[r0:r0 + q, :], precision=HIGHEST,
                       preferred_element_type=F32)
        a_cs_t = a_cs.T
        dt_t = dt.T
        last = a_cs[q - 1:q, :]
        sd_x = _expand_heads(jnp.exp(a_cs), left)
        w_x = _expand_heads(dt * jnp.exp(last - a_cs), left)
        cd_x = _expand_heads(jnp.exp(last), left[0:1, :])
        decays.append((sd_x, w_x, cd_x))
        for h in range(SSD_HEADS):
            decay = jnp.exp(a_cs[:, h:h + 1] - a_cs_t[h:h + 1, :])
            dec_ref[ci * SSD_HEADS + h] = jnp.where(causal, decay * dt_t[h:h + 1, :], 0.0)

    smat = _shift_matrix(tq, 3)

    for c0 in range(0, CONV_GROUP_W, CONV_CHUNK):
        cols = slice(c0, c0 + CONV_CHUNK)
        c_b, v_b = uc_ref[:, cols], uv_ref[:, cols]
        c_f, v_f = c_b.astype(F32), v_b.astype(F32)
        c1, c2 = _shifted_rows(smat, c_b, c_f, tail_c, cols, 2)
        v1, v2 = _shifted_rows(smat, v_b, v_f, tail_v, cols, 2)
        conv = (cw_ref[2:3, cols] * (c_f * v_f) + cw_ref[1:2, cols] * (c1 * v1)
                + cw_ref[0:1, cols] * (c2 * v2))
        o_ref[:, cols] = (ub_ref[:, cols].astype(F32) * conv).astype(BF16)

    _conv4_silu(smat, uxs_ref, tail_xs, wxs_ref, bxs_ref, xs_ref)
    _conv4_silu(smat, ubc_ref, tail_bc, wbc_ref, bbc_ref, bc_ref)
    xs_all = xs_ref[...]
    bc_all = bc_ref[...]
    gs = SSD_GROUPS * SSD_STATE

    y_chunks = []
    for ci, r0 in enumerate(range(0, tq, q)):
        xs = xs_all[r0:r0 + q, :]
        bm = bc_all[r0:r0 + q, :gs]
        cm = bc_all[r0:r0 + q, gs:]
        sd_x, w_x, cd_x = decays[ci]
        xs_b = xs.astype(BF16)
        xw_b = (xs * w_x).astype(BF16)

        y_parts = []
        for g in range(SSD_GROUPS):
            cm_g = cm[:, g * SSD_STATE:(g + 1) * SSD_STATE].astype(BF16)
            bm_g = bm[:, g * SSD_STATE:(g + 1) * SSD_STATE]
            cb = lax.dot_general(cm_g, bm_g.astype(BF16), NT_DIMS, preferred_element_type=F32)
            s_prev = state_ref[g]
            y_off = jnp.dot(cm_g, s_prev.astype(BF16), preferred_element_type=F32)
            y_off = y_off * sd_x[:, g * gw:(g + 1) * gw]
            for pr in range(hpg // 2):
                h0 = ci * SSD_HEADS + g * hpg + pr * 2
                lhs = jnp.concatenate([(cb * dec_ref[h0]).astype(BF16),
                                       (cb * dec_ref[h0 + 1]).astype(BF16)], axis=1)
                c0 = g * gw + pr * LANES
                xp = xs_b[:, c0:c0 + LANES]
                zero = jnp.zeros_like(xp)
                rhs = jnp.concatenate([jnp.where(left, xp, zero), jnp.where(left, zero, xp)], axis=0)
                y_parts.append(jnp.dot(lhs, rhs, preferred_element_type=F32)
                               + y_off[:, pr * LANES:(pr + 1) * LANES])
            upd = jnp.dot(bm_g.T.astype(BF16), xw_b[:, g * gw:(g + 1) * gw],
                          preferred_element_type=F32)
            state_ref[g] = s_prev * cd_x[:, g * gw:(g + 1) * gw] + upd
        y_chunks.append(jnp.concatenate(y_parts, axis=1))
    y = jnp.concatenate(y_chunks, axis=0) + xs_all * dskip_ref[...]

    yg = y * _silu(uz_ref[...].astype(F32))
    outs = []
    for g in range(SSD_GROUPS):
        v = yg[:, g * gw:(g + 1) * gw]
        ms = jnp.mean(v * v, axis=-1, keepdims=True)
        outs.append(v * lax.rsqrt(ms + EPS))
    y_ssd = jnp.concatenate(outs, axis=1) * nw_ref[...]
    o_ref[:, CONV_GROUP_W:] = y_ssd.astype(BF16)


def _mix(u, dt_raw, conv_w, wxs, wbc, bxs, bbc, dtb, alog, dskip_x, nw):
    t = u.shape[0]
    tq = MIX_TILE
    gw = CONV_GROUP_W
    bcw = 2 * SSD_GROUPS * SSD_STATE
    const = lambda i: (0, 0)
    ublk = lambda k: pl.BlockSpec((tq, gw), lambda i, k=k: (i, k))
    full = lambda arr: pl.BlockSpec(arr.shape, const)
    return pl.pallas_call(
        _mix_kernel,
        grid=(t // tq,),
        in_specs=[ublk(0), ublk(1), ublk(2), ublk(3), ublk(4),
                  pl.BlockSpec((tq, bcw), lambda i: (i, 5 * gw // bcw)),
                  pl.BlockSpec((tq, LANES), lambda i: (i, 0)),
                  full(conv_w), full(wxs), full(wbc), full(bxs), full(bbc), full(dtb), full(alog),
                  full(dskip_x), full(nw)],
        out_specs=pl.BlockSpec((tq, 2 * gw), lambda i: (i, 0)),
        out_shape=jax.ShapeDtypeStruct((t, 2 * gw), BF16),
        scratch_shapes=[pltpu.VMEM((SUBLANES, gw), F32), pltpu.VMEM((SUBLANES, gw), F32),
                        pltpu.VMEM((SUBLANES, gw), F32), pltpu.VMEM((SUBLANES, bcw), F32),
                        pltpu.VMEM((SSD_GROUPS, SSD_STATE, gw // SSD_GROUPS), F32),
                        pltpu.VMEM((tq // SSD_CHUNK * SSD_HEADS, SSD_CHUNK, SSD_CHUNK), F32),
                        pltpu.VMEM((tq, gw), F32), pltpu.VMEM((tq, bcw), F32)],
        compiler_params=_cparams(("arbitrary",)),
        name="mix",
    )(u, u, u, u, u, u, dt_raw, conv_w, wxs, wbc, bxs, bbc, dtb, alog, dskip_x, nw)


def _outproj_kernel(y_ref, x_ref, w_ref, g1_ref, nw_ref, sc_ref, sh_ref, wrh_ref, br_ref,
                    x1_ref, h_ref, rt_ref, route_ref, acc_ref, hb_ref, hl_ref, *, rows):
    tm, d = x_ref.shape
    scale = nw_ref[...] * (1.0 + sc_ref[...])
    acc_ref[...] = jnp.dot(y_ref[...], w_ref[...], preferred_element_type=F32)
    for r0 in range(0, tm, rows):
        x1 = x_ref[r0:r0 + rows, :] + g1_ref[...] * acc_ref[r0:r0 + rows, :]
        x1_ref[r0:r0 + rows, :] = x1
        ms = jnp.mean(x1 * x1, axis=-1, keepdims=True)
        h = x1 * lax.rsqrt(ms + EPS) * scale + sh_ref[...]
        h_ref[r0:r0 + rows, :] = h
        hb = h.astype(BF16)
        hb_ref[r0:r0 + rows, :] = hb
        hl_ref[r0:r0 + rows, :] = (h - hb.astype(F32)).astype(BF16)

    both = lax.dot_general(wrh_ref[...], hb_ref[...], NT_DIMS, preferred_element_type=F32)
    lt = (both[:ROUTER_ROWS, :] + both[ROUTER_ROWS:, :]
          + lax.dot_general(wrh_ref[:ROUTER_ROWS, :], hl_ref[...], NT_DIMS, preferred_element_type=F32)
          + br_ref[...])
    sub = lax.broadcasted_iota(I32, (ROUTER_ROWS, tm), 0).astype(F32)
    ninf = jnp.float32(-jnp.inf)
    big = jnp.float32(1e9)
    is_g = (sub >= N_EXPERTS) & (sub < N_EXPERTS + N_EXPERT_GROUPS)
    gl = jnp.where(is_g, lt, ninf)
    gmax = jnp.max(gl, axis=0, keepdims=True)
    grp_p = 1.0 / jnp.sum(jnp.exp(gl - gmax), axis=0, keepdims=True)
    gidx = jnp.min(jnp.where(gl == gmax, sub - N_EXPERTS, big), axis=0, keepdims=True)
    lo = gidx * EXPERTS_PER_GROUP
    el = jnp.where((sub >= lo) & (sub < lo + EXPERTS_PER_GROUP), lt, ninf)
    m1 = jnp.max(el, axis=0, keepdims=True)
    i1 = jnp.min(jnp.where(el == m1, sub, big), axis=0, keepdims=True)
    el2 = jnp.where(sub == i1, ninf, el)
    m2 = jnp.max(el2, axis=0, keepdims=True)
    i2 = jnp.min(jnp.where(el2 == m2, sub, big), axis=0, keepdims=True)
    e = jnp.exp(m2 - m1)
    w1 = grp_p / (1.0 + e)
    w2 = grp_p * e / (1.0 + e)
    rt = jnp.where(sub == 0, i1, jnp.where(sub == 1, i2,
                   jnp.where(sub == 2, w1, jnp.where(sub == 3, w2, 0.0))))
    for c, r0 in enumerate(range(0, tm, rows)):
        rt_ref[c] = rt[:SUBLANES, r0:r0 + rows]
    route_ref[...] = jnp.concatenate([rt, jnp.zeros((LANES - ROUTER_ROWS, tm), F32)], axis=0).T


def _outproj(ycat, x, w_out, g1, nw, sc, sh, wr_hilo, b_r):
    t, d = x.shape
    tm = ROW_TILE
    rows = POS_CHUNK
    const = lambda i: (0, 0)
    row = pl.BlockSpec((1, d), const)
    tile = pl.BlockSpec((tm, d), lambda i: (i, 0))
    return pl.pallas_call(
        functools.partial(_outproj_kernel, rows=rows),
        grid=(t // tm,),
        in_specs=[tile, tile,
                  pl.BlockSpec((d, d), const, pipeline_mode=pl.Buffered(1)), row, row, row, row,
                  pl.BlockSpec((2 * ROUTER_ROWS, d), const),
                  pl.BlockSpec((ROUTER_ROWS, 1), const)],
        out_specs=[tile, tile,
                   pl.BlockSpec((tm // rows, SUBLANES, rows), lambda i: (i, 0, 0)),
                   pl.BlockSpec((tm, LANES), lambda i: (i, 0))],
        out_shape=[jax.ShapeDtypeStruct((t, d), F32), jax.ShapeDtypeStruct((t, d), F32),
                   jax.ShapeDtypeStruct((t // rows, SUBLANES, rows), F32),
                   jax.ShapeDtypeStruct((t, LANES), F32)],
        scratch_shapes=[pltpu.VMEM((tm, d), F32), pltpu.VMEM((tm, d), BF16), pltpu.VMEM((tm, d), BF16)],
        compiler_params=_cparams(("arbitrary",)),
        name="outproj",
    )(ycat, x, w_out, g1, nw, sc, sh, wr_hilo, b_r)


def _meta_kernel(rt_ref, pos_ref, meta_ref):
    nchunk, _, c = rt_ref.shape
    r = float(MOE_ROWS)
    sub = lax.broadcasted_iota(I32, (LANES, c), 0).astype(F32)

    def onehots(k):
        rt = rt_ref[k]
        return (sub == rt[0:1, :]).astype(F32), (sub == rt[1:2, :]).astype(F32)

    def count(k, cnt):
        oh1, oh2 = onehots(k)
        return cnt + jnp.sum(oh1 + oh2, axis=1, keepdims=True)

    cnt = lax.fori_loop(0, nchunk, count, jnp.zeros((LANES, 1), F32))
    padded = jnp.floor((cnt + (r - 1.0)) * (1.0 / r)) * r
    er = lax.broadcasted_iota(I32, (LANES, LANES), 0)
    ec = lax.broadcasted_iota(I32, (LANES, LANES), 1)
    strict_lower = (ec < er).astype(F32)
    pstart = jnp.dot(strict_lower, jnp.broadcast_to(padded, (LANES, LANES)), precision=HIGHEST,
                     preferred_element_type=F32)[:, 0:1]
    tr = lax.broadcasted_iota(I32, (c, c), 0)
    tc = lax.broadcasted_iota(I32, (c, c), 1)
    before = (tr < tc).astype(BF16)

    def place(k, run):
        oh1, oh2 = onehots(k)
        both = oh1 + oh2
        prior = jnp.dot(both.astype(BF16), before, preferred_element_type=F32)
        slot = prior + (pstart + run)
        p1 = jnp.sum(oh1 * slot, axis=0, keepdims=True)
        p2 = jnp.sum(oh2 * slot, axis=0, keepdims=True)
        pos_ref[k] = jnp.concatenate([p1, p2], axis=0).astype(I32)
        return run + jnp.sum(both, axis=1, keepdims=True)

    lax.fori_loop(0, nchunk, place, jnp.zeros((LANES, 1), F32))

    nbl = META_LANES
    e_sub = lax.broadcasted_iota(I32, (LANES, nbl), 0).astype(F32)
    blk0 = lax.broadcasted_iota(I32, (LANES, nbl), 1).astype(F32) * r
    pend = pstart + padded
    bexp = jnp.sum(jnp.where((e_sub < N_EXPERTS) & (pend <= blk0), 1.0, 0.0), axis=0, keepdims=True)
    bexp = jnp.minimum(bexp, N_EXPERTS - 1.0)
    cend = jnp.sum(jnp.where(e_sub == bexp, pstart + cnt, 0.0), axis=0, keepdims=True)
    nvalid = jnp.clip(cend - blk0[0:1, :], 0.0, r)
    total = jnp.sum(padded, axis=0, keepdims=True)
    used = blk0[0:1, :] < total
    nvalid = jnp.where(used, nvalid, 0.0)
    pend_row = jnp.sum(jnp.where(er == ec, pend, 0.0), axis=0, keepdims=True)
    b_sub = lax.broadcasted_iota(I32, (nbl, LANES), 0).astype(F32) * r
    e_lane = lax.broadcasted_iota(I32, (nbl, LANES), 1)
    bexp_col = jnp.sum(jnp.where((e_lane < N_EXPERTS) & (pend_row <= b_sub), 1.0, 0.0),
                       axis=1, keepdims=True)
    bexp_col = jnp.minimum(bexp_col, N_EXPERTS - 1.0)
    seg_end = jnp.sum(jnp.where(e_sub == bexp, pend, 0.0), axis=0, keepdims=True)
    bp_sub = lax.broadcasted_iota(I32, (nbl, nbl), 0).astype(F32) * r
    nxt = jnp.sum(jnp.where(bp_sub == seg_end, bexp_col, 0.0), axis=0, keepdims=True)
    nxt = jnp.where(used & (seg_end < total), nxt, -1.0)
    rowsel = lax.broadcasted_iota(I32, (SUBLANES, nbl), 0)
    meta = jnp.where(rowsel == 0, bexp, jnp.where(rowsel == 1, nvalid, jnp.where(rowsel == 2, nxt, 0.0)))
    meta_ref[...] = meta.astype(I32)


def _meta(rt):
    nchunk, _, c = rt.shape
    return pl.pallas_call(
        _meta_kernel,
        out_shape=[jax.ShapeDtypeStruct((nchunk, 2, c), I32),
                   jax.ShapeDtypeStruct((SUBLANES, META_LANES), I32)],
        compiler_params=pltpu.CompilerParams(vmem_limit_bytes=VMEM_LIMIT),
        name="meta",
    )(rt)


def _dispatch_kernel(meta_ref, pos_ref, h_ref, hs_hbm, zbuf, sem, zsem):
    i = pl.program_id(0)
    tm = h_ref.shape[0]
    nck, _, c = pos_ref.shape
    r = MOE_ROWS

    @pl.when(i == 0)
    def _():
        zbuf[...] = jnp.zeros_like(zbuf)

        def zero_copy(b):
            return pltpu.make_async_copy(zbuf, hs_hbm.at[pl.ds(pl.multiple_of(b * r, r), r)], zsem)

        def zstart(b, carry):
            @pl.when(meta_ref[META_NVALID + b] < r)
            def _():
                zero_copy(b).start()
            return carry

        def zwait(b, carry):
            @pl.when(meta_ref[META_NVALID + b] < r)
            def _():
                zero_copy(b).wait()
            return carry

        lax.fori_loop(0, hs_hbm.shape[0] // r, zstart, 0)
        lax.fori_loop(0, hs_hbm.shape[0] // r, zwait, 0)

    for ck in range(nck):
        for j in range(c):
            row = ck * c + j
            for k in range(2):
                pltpu.make_async_copy(h_ref.at[pl.ds(row, 1)],
                                      hs_hbm.at[pl.ds(pos_ref[ck, k, j], 1)], sem).start(priority=k)
    for k in range(2):
        pltpu.make_async_copy(h_ref, hs_hbm.at[pl.ds(0, tm)], sem).wait()


def _dispatch(meta_flat, pos, h, n_slots):
    t, d = h.shape
    tm = DISPATCH_TILE
    nck = tm // POS_CHUNK
    grid_spec = pltpu.PrefetchScalarGridSpec(
        num_scalar_prefetch=1,
        grid=(t // tm,),
        in_specs=[pl.BlockSpec((nck, 2, POS_CHUNK), lambda i, m: (i, 0, 0), memory_space=pltpu.SMEM),
                  pl.BlockSpec((tm, d), lambda i, m: (i, 0))],
        out_specs=pl.BlockSpec(memory_space=pl.ANY),
        scratch_shapes=[pltpu.VMEM((MOE_ROWS, d), F32), pltpu.SemaphoreType.DMA(()),
                        pltpu.SemaphoreType.DMA(())],
    )
    return pl.pallas_call(
        _dispatch_kernel,
        grid_spec=grid_spec,
        out_shape=jax.ShapeDtypeStruct((n_slots, d), F32),
        compiler_params=_cparams(("arbitrary",)),
        name="dispatch",
    )(meta_flat, pos, h)


def _moe_kernel(meta_ref, x_ref, wg_hbm, wu_hbm, wd_hbm, y_ref, stg_g, stg_u, stg_d, wgb, wub, wdb, sems):
    b = pl.program_id(0)
    e = meta_ref[META_EXPERT + b]
    nxt = meta_ref[META_NEXT + b]
    used = meta_ref[META_NVALID + b] > 0
    prev = meta_ref[META_EXPERT + jnp.maximum(b - 1, 0)]
    changed = jnp.logical_and(used, jnp.logical_or(b == 0, e != prev))
    d, f = wgb.shape
    kc = 512

    def copies(ex):
        return [pltpu.make_async_copy(w.at[ex], stg, sems.at[k])
                for k, (w, stg) in enumerate(((wg_hbm, stg_g), (wu_hbm, stg_u), (wd_hbm, stg_d)))]

    @pl.when(b == 0)
    def _():
        for cp in copies(e):
            cp.start()

    def convert_dot(lhs, stg, wb, n):
        acc = None
        for k0 in range(0, n, kc):
            w = stg[k0:k0 + kc, :].astype(BF16)
            wb[k0:k0 + kc, :] = w
            part = jnp.dot(lhs[:, k0:k0 + kc], w, preferred_element_type=F32)
            acc = part if acc is None else acc + part
        return acc

    @pl.when(changed)
    def _():
        cps = copies(e)
        x = x_ref[...].astype(BF16)
        cps[0].wait()
        gate = convert_dot(x, stg_g, wgb, d)
        cps[1].wait()
        up = convert_dot(x, stg_u, wub, d)
        hid = (_silu(gate) * up).astype(BF16)
        cps[2].wait()
        y_ref[...] = convert_dot(hid, stg_d, wdb, f)

        @pl.when(nxt >= 0)
        def _():
            for cp in copies(nxt):
                cp.start(priority=1)

    @pl.when(jnp.logical_and(used, jnp.logical_not(changed)))
    def _():
        x = x_ref[...].astype(BF16)
        gate = jnp.dot(x, wgb[...], preferred_element_type=F32)
        up = jnp.dot(x, wub[...], preferred_element_type=F32)
        hid = (_silu(gate) * up).astype(BF16)
        y_ref[...] = jnp.dot(hid, wdb[...], preferred_element_type=F32)

    @pl.when(jnp.logical_not(used))
    def _():
        y_ref[...] = jnp.zeros_like(y_ref)


def _moe(meta_flat, hs, w_gate, w_up, w_down):
    n_slots, d = hs.shape
    f = w_gate.shape[2]
    hbm = pl.BlockSpec(memory_space=pl.ANY)
    grid_spec = pltpu.PrefetchScalarGridSpec(
        num_scalar_prefetch=1,
        grid=(n_slots // MOE_ROWS,),
        in_specs=[pl.BlockSpec((MOE_ROWS, d), lambda b, m: (jnp.where(m[META_NVALID + b] > 0, b, 0), 0)),
                  hbm, hbm, hbm],
        out_specs=pl.BlockSpec((MOE_ROWS, d), lambda b, m: (b, 0)),
        scratch_shapes=[pltpu.VMEM((d, f), F32), pltpu.VMEM((d, f), F32), pltpu.VMEM((f, d), F32),
                        pltpu.VMEM((d, f), BF16), pltpu.VMEM((d, f), BF16), pltpu.VMEM((f, d), BF16),
                        pltpu.SemaphoreType.DMA((3,))],
    )
    return pl.pallas_call(
        _moe_kernel,
        grid_spec=grid_spec,
        out_shape=jax.ShapeDtypeStruct((n_slots, d), F32),
        compiler_params=_cparams(("arbitrary",)),
        name="moe",
    )(meta_flat, hs, w_gate, w_up, w_down)


def _final_kernel(pos_ref, pos1_ref, pos2_ref, y_hbm, x1_ref, route_ref, g2_ref, nw_ref, o_ref, ybuf, sems):
    i = pl.program_id(0)
    nt = pl.num_programs(0)
    tm = x1_ref.shape[0]
    slot = lax.rem(i, 3)

    def gather(p_ref, s, rows):
        for j in rows:
            for k in range(2):
                pltpu.make_async_copy(y_hbm.at[pl.ds(p_ref[0, k, j], 1)],
                                      ybuf.at[s, k, pl.ds(j, 1)], sems.at[s]).start(priority=k)

    def drain(s):
        for k in range(2):
            pltpu.make_async_copy(y_hbm.at[pl.ds(0, tm)], ybuf.at[s, k], sems.at[s]).wait()

    @pl.when(i == 0)
    def _():
        gather(pos_ref, 0, range(tm))
        gather(pos1_ref, 1, range(tm))

    def step(s):
        drain(s)
        rc = tm // 8
        for r0 in range(0, tm, rc):
            rt = route_ref[r0:r0 + rc, :]
            moe = rt[:, 2:3] * ybuf[s, 0, r0:r0 + rc, :] + rt[:, 3:4] * ybuf[s, 1, r0:r0 + rc, :]
            x2 = x1_ref[r0:r0 + rc, :] + g2_ref[...] * moe
            ms = jnp.mean(x2 * x2, axis=-1, keepdims=True)
            o_ref[r0:r0 + rc, :] = x2 * lax.rsqrt(ms + EPS) * nw_ref[...]
            gather(pos2_ref, (s + 2) % 3, range(r0, r0 + rc))

    for s in range(3):
        @pl.when(slot == s)
        def _(s=s):
            step(s)

    @pl.when(i == nt - 1)
    def _():
        drain(lax.rem(i + 1, 3))
        drain(lax.rem(i + 2, 3))


def _final(y_sorted, pos, x1, route, g2, nw):
    t, d = x1.shape
    tm = POS_CHUNK
    nt = t // tm
    const = lambda i: (0, 0)
    ahead = lambda k: pl.BlockSpec((1, 2, tm), lambda i: (jnp.minimum(i + k, nt - 1), 0, 0),
                                   memory_space=pltpu.SMEM)
    return pl.pallas_call(
        _final_kernel,
        grid=(nt,),
        in_specs=[ahead(0), ahead(1), ahead(2),
                  pl.BlockSpec(memory_space=pl.ANY),
                  pl.BlockSpec((tm, d), lambda i: (i, 0)),
                  pl.BlockSpec((tm, LANES), lambda i: (i, 0)),
                  pl.BlockSpec((1, d), const), pl.BlockSpec((1, d), const)],
        out_specs=pl.BlockSpec((tm, d), lambda i: (i, 0)),
        out_shape=jax.ShapeDtypeStruct((t, d), F32),
        scratch_shapes=[pltpu.VMEM((3, 2, tm, d), F32), pltpu.SemaphoreType.DMA((3,))],
        compiler_params=_cparams(("arbitrary",)),
        name="final",
    )(pos, pos, pos, y_sorted, x1, route, g2, nw)


def kernel(x, c, w_ada, b_ada, norm1_w, w_in, conv_w, ssd_conv_w, ssd_conv_b, dt_bias, a_log, d_skip,
           ssd_norm_w, w_out, norm2_w, w_router_grp, b_router_grp, w_router_exp, b_router_exp,
           w_gate, w_up, w_down, final_norm_w):
    bsz, seq, d = x.shape
    assert bsz == 1 and w_ada.shape[0] == 1
    t = bsz * seq
    assert t % DISPATCH_TILE == 0 and DISPATCH_TILE % ROW_TILE == 0
    xt = x.reshape(t, d)
    gw = CONV_GROUP_W
    n_main = 5 * gw + 2 * SSD_GROUPS * SSD_STATE
    n_heads = w_in.shape[2] - n_main
    assert n_heads == SSD_HEADS

    mod = _ada(c, w_ada[0], b_ada[0])
    sh1, sc1, g1, sh2, sc2, g2 = [mod[:, k * d:(k + 1) * d] for k in range(6)]

    w_all = w_in[0].astype(BF16)
    w_dt = jnp.pad(w_all[:, n_main:], ((0, 0), (0, LANES - n_heads)))
    u, dt_raw = _inproj(xt, norm1_w, sc1, sh1, w_all, w_dt, n_main)

    pad_h = lambda v: jnp.pad(v, ((0, 0), (0, LANES - n_heads)))
    ycat = _mix(u, dt_raw, conv_w[0], ssd_conv_w[0][:, :gw], ssd_conv_w[0][:, gw:],
                ssd_conv_b[:, :gw], ssd_conv_b[:, gw:], pad_h(dt_bias), pad_h(a_log),
                jnp.repeat(d_skip, SSD_HEAD_DIM, axis=1), ssd_norm_w)

    n_r = N_EXPERTS + N_EXPERT_GROUPS
    w_rt = jnp.pad(jnp.concatenate([w_router_exp[0], w_router_grp[0]], axis=1).T,
                   ((0, ROUTER_ROWS - n_r), (0, 0)))
    wr_hi = w_rt.astype(BF16)
    wr_lo = (w_rt - wr_hi.astype(F32)).astype(BF16)
    b_r = jnp.pad(jnp.concatenate([b_router_exp, b_router_grp], axis=1),
                  ((0, 0), (0, ROUTER_ROWS - n_r))).reshape(ROUTER_ROWS, 1)
    x1, h2, rt, route = _outproj(ycat, xt, w_out[0].astype(BF16), g1, norm2_w, sc2, sh2,
                                 jnp.concatenate([wr_hi, wr_lo], axis=0), b_r)

    pos, meta = _meta(rt)
    meta_flat = meta.reshape(SUBLANES * META_LANES)
    n_blocks = (2 * t + N_EXPERTS * (MOE_ROWS - 1) + MOE_ROWS - 1) // MOE_ROWS
    assert n_blocks <= META_LANES
    hs = _dispatch(meta_flat, pos, h2, n_blocks * MOE_ROWS)
    y_sorted = _moe(meta_flat, hs, w_gate[0], w_up[0], w_down[0])
    out = _final(y_sorted, pos, x1, route, g2, final_norm_w.reshape(1, d))
    return out.reshape(bsz, seq, d)
```

```python
import functools

import jax
import jax.numpy as jnp
from jax import lax
from jax.experimental import pallas as pl
from jax.experimental.pallas import tpu as pltpu

F32 = jnp.float32
BF16 = jnp.bfloat16
U32 = jnp.uint32
I32 = jnp.int32
EPS = 1e-6

CONV_GROUP_W = 1024
SSD_HEADS = 16
SSD_HEAD_DIM = 64
SSD_GROUPS = 2
SSD_STATE = 128
N_EXPERT_GROUPS = 4
EXPERTS_PER_GROUP = 8
N_EXPERTS = 32
LANES = 128
SUBLANES = 8
VMEM_LIMIT = 56 * 1024 * 1024

SSD_CHUNK = 128
MIX_TILE = 256
CONV_CHUNK = 256
ROW_TILE = 512
DISPATCH_TILE = 1024
POS_CHUNK = 256
MOE_ROWS = 256
ROUTER_ROWS = 48
META_LANES = 256
META_EXPERT, META_NVALID, META_NEXT = 0, META_LANES, 2 * META_LANES
HIGHEST = lax.Precision.HIGHEST
NT_DIMS = (((1,), (1,)), ((), ()))


def _silu(v):
    return 0.5 * v * (1.0 + jnp.tanh(0.5 * v))


def _cparams(sem):
    return pltpu.CompilerParams(dimension_semantics=sem, vmem_limit_bytes=VMEM_LIMIT)


def _ada_kernel(c_ref, w_ref, b_ref, o_ref):
    s = _silu(c_ref[...])
    o_ref[...] = jnp.sum(w_ref[...] * s, axis=0, keepdims=True) + b_ref[...]


def _ada(c, w_ada, b_ada):
    d, n = w_ada.shape
    tn = 1024
    return pl.pallas_call(
        _ada_kernel,
        grid=(n // tn,),
        in_specs=[pl.BlockSpec((d, 1), lambda j: (0, 0)),
                  pl.BlockSpec((d, tn), lambda j: (0, j)),
                  pl.BlockSpec((1, tn), lambda j: (0, j))],
        out_specs=pl.BlockSpec((1, tn), lambda j: (0, j)),
        out_shape=jax.ShapeDtypeStruct((1, n), F32),
        compiler_params=_cparams(("arbitrary",)),
        name="ada",
    )(c.reshape(d, 1), w_ada, b_ada.reshape(1, n))


def _inproj_kernel(x_ref, nw_ref, sc_ref, sh_ref, w_ref, wdt_ref, u_ref, dt_ref, h_ref, *, rows, tn):
    tm = x_ref.shape[0]
    n = w_ref.shape[1]
    scale = nw_ref[...] * (1.0 + sc_ref[...])
    for r0 in range(0, tm, rows):
        xf = x_ref[r0:r0 + rows, :]
        ms = jnp.mean(xf * xf, axis=-1, keepdims=True)
        h_ref[r0:r0 + rows, :] = (xf * lax.rsqrt(ms + EPS) * scale + sh_ref[...]).astype(BF16)
        h = h_ref[r0:r0 + rows, :]
        dt_ref[r0:r0 + rows, :] = jnp.dot(h, wdt_ref[...], preferred_element_type=F32)
        for n0 in range(0, n, tn):
            u_ref[r0:r0 + rows, n0:n0 + tn] = jnp.dot(
                h, w_ref[:, n0:n0 + tn], preferred_element_type=F32).astype(BF16)


def _inproj(x, nw, sc, sh, w_all, w_dt, n):
    t, d = x.shape
    tm = ROW_TILE
    const = lambda i: (0, 0)
    row = pl.BlockSpec((1, d), const)
    return pl.pallas_call(
        functools.partial(_inproj_kernel, rows=256, tn=512),
        grid=(t // tm,),
        in_specs=[pl.BlockSpec((tm, d), lambda i: (i, 0)), row, row, row,
                  pl.BlockSpec((d, n), const, pipeline_mode=pl.Buffered(1)),
                  pl.BlockSpec((d, LANES), const)],
        out_specs=[pl.BlockSpec((tm, n), lambda i: (i, 0)),
                   pl.BlockSpec((tm, LANES), lambda i: (i, 0))],
        out_shape=[jax.ShapeDtypeStruct((t, n), BF16), jax.ShapeDtypeStruct((t, LANES), F32)],
        scratch_shapes=[pltpu.VMEM((tm, d), BF16)],
        compiler_params=_cparams(("arbitrary",)),
        name="inproj",
    )(x, nw, sc, sh, w_all, w_dt)


def _shift_matrix(q, nshift):
    rr = lax.broadcasted_iota(I32, (q, q), 0)
    cc = lax.broadcasted_iota(I32, (q, q), 1)
    return jnp.concatenate([(rr - cc == s).astype(BF16) for s in range(1, nshift + 1)], axis=0)


def _shifted_rows(smat, x_b, x_f, tail_ref, cols, nshift):
    q = x_b.shape[0]
    sh = jnp.dot(smat[:nshift * q, :], x_b, preferred_element_type=F32)
    head = jnp.concatenate([tail_ref[:, cols], x_f[0:SUBLANES, :]], axis=0)
    outs = [jnp.concatenate([head[SUBLANES - s:2 * SUBLANES - s, :],
                             sh[(s - 1) * q + SUBLANES:s * q, :]], axis=0)
            for s in range(1, nshift + 1)]
    tail_ref[:, cols] = x_f[q - SUBLANES:, :]
    return outs


def _conv4_silu(smat, x_ref, tail_ref, w_ref, b_ref, out_ref):
    for c0 in range(0, x_ref.shape[1], CONV_CHUNK):
        cols = slice(c0, c0 + CONV_CHUNK)
        x_b = x_ref[:, cols]
        x_f = x_b.astype(F32)
        s1, s2, s3 = _shifted_rows(smat, x_b, x_f, tail_ref, cols, 3)
        conv = (w_ref[3:4, cols] * x_f + w_ref[2:3, cols] * s1 + w_ref[1:2, cols] * s2
                + w_ref[0:1, cols] * s3)
        out_ref[:, cols] = _silu(conv + b_ref[:, cols])


def _expand_heads(v, left):
    parts = [jnp.where(left, v[:, 2 * k:2 * k + 1], v[:, 2 * k + 1:2 * k + 2])
             for k in range(SSD_HEADS // 2)]
    return jnp.concatenate(parts, axis=1)


def _mix_kernel(ub_ref, uc_ref, uv_ref, uz_ref, uxs_ref, ubc_ref, dt_ref,
                cw_ref, wxs_ref, wbc_ref, bxs_ref, bbc_ref, dtb_ref, alog_ref, dskip_ref,
                nw_ref, o_ref,
                tail_c, tail_v, tail_xs, tail_bc, state_ref, dec_ref, xs_ref, bc_ref):
    i = pl.program_id(0)
    tq = ub_ref.shape[0]
    q = SSD_CHUNK
    gw = SSD_HEADS // SSD_GROUPS * SSD_HEAD_DIM
    hpg = SSD_HEADS // SSD_GROUPS

    @pl.when(i == 0)
    def _():
        for tail in (tail_c, tail_v, tail_xs, tail_bc):
            tail[...] = jnp.zeros_like(tail)
        state_ref[...] = jnp.zeros_like(state_ref)

    dtv = dt_ref[...] + dtb_ref[...]
    dt_all = jnp.maximum(dtv, 0.0) + jnp.log1p(jnp.exp(-jnp.abs(dtv)))
    a_all = dt_all * (-jnp.exp(alog_ref[...]))
    rr = lax.broadcasted_iota(I32, (q, q), 0)
    cc = lax.broadcasted_iota(I32, (q, q), 1)
    causal = rr >= cc
    tri = causal.astype(F32)
    left = lax.broadcasted_iota(I32, (q, LANES), 1) < SSD_HEAD_DIM
    decays = []
    for ci, r0 in enumerate(range(0, tq, q)):
        dt = dt_all[r0:r0 + q, :]
        a_cs = jnp.dot(tri, a_all[r0:r0 + q, :], precision=HIGHEST,
                       preferred_element_type=F32)
        a_cs_t = a_cs.T
        dt_t = dt.T
        last = a_cs[q - 1:q, :]
        sd_x = _expand_heads(jnp.exp(a_cs), left)
        w_x = _expand_heads(dt * jnp.exp(last - a_cs), left)
        cd_x = _expand_heads(jnp.exp(last), left[0:1, :])
        decays.append((sd_x, w_x, cd_x))
        for h in range(SSD_HEADS):
            decay = jnp.exp(a_cs[:, h:h + 1] - a_cs_t[h:h + 1, :])
            dec_ref[ci * SSD_HEADS + h] = jnp.where(causal, decay * dt_t[h:h + 1, :], 0.0)

    smat = _shift_matrix(tq, 3)

    for c0 in range(0, CONV_GROUP_W, CONV_CHUNK):
        cols = slice(c0, c0 + CONV_CHUNK)
        c_b, v_b = uc_ref[:, cols], uv_ref[:, cols]
        c_f, v_f = c_b.astype(F32), v_b.astype(F32)
        c1, c2 = _shifted_rows(smat, c_b, c_f, tail_c, cols, 2)
        v1, v2 = _shifted_rows(smat, v_b, v_f, tail_v, cols, 2)
        conv = (cw_ref[2:3, cols] * (c_f * v_f) + cw_ref[1:2, cols] * (c1 * v1)
                + cw_ref[0:1, cols] * (c2 * v2))
        o_ref[:, cols] = (ub_ref[:, cols].astype(F32) * conv).astype(BF16)

    _conv4_silu(smat, uxs_ref, tail_xs, wxs_ref, bxs_ref, xs_ref)
    _conv4_silu(smat, ubc_ref, tail_bc, wbc_ref, bbc_ref, bc_ref)
    xs_all = xs_ref[...]
    bc_all = bc_ref[...]
    gs = SSD_GROUPS * SSD_STATE

    y_chunks = []
    for ci, r0 in enumerate(range(0, tq, q)):
        xs = xs_all[r0:r0 + q, :]
        bm = bc_all[r0:r0 + q, :gs]
        cm = bc_all[r0:r0 + q, gs:]
        sd_x, w_x, cd_x = decays[ci]
        xs_b = xs.astype(BF16)
        xw_b = (xs * w_x).astype(BF16)

        y_parts = []
        for g in range(SSD_GROUPS):
            cm_g = cm[:, g * SSD_STATE:(g + 1) * SSD_STATE].astype(BF16)
            bm_g = bm[:, g * SSD_STATE:(g + 1) * SSD_STATE]
            cb = lax.dot_general(cm_g, bm_g.astype(BF16), NT_DIMS, preferred_element_type=F32)
            s_prev = state_ref[g]
            y_off = jnp.dot(cm_g, s_prev.astype(BF16), preferred_element_type=F32)
            y_off = y_off * sd_x[:, g * gw:(g + 1) * gw]
            for pr in range(hpg // 2):
                h0 = ci * SSD_HEADS + g * hpg + pr * 2
                lhs = jnp.concatenate([(cb * dec_ref[h0]).astype(BF16),
                                       (cb * dec_ref[h0 + 1]).astype(BF16)], axis=1)
                c0 = g * gw + pr * LANES
                xp = xs_b[:, c0:c0 + LANES]
                zero = jnp.zeros_like(xp)
                rhs = jnp.concatenate([jnp.where(left, xp, zero), jnp.where(left, zero, xp)], axis=0)
                y_parts.append(jnp.dot(lhs, rhs, preferred_element_type=F32)
                               + y_off[:, pr * LANES:(pr + 1) * LANES])
            upd = jnp.dot(bm_g.T.astype(BF16), xw_b[:, g * gw:(g + 1) * gw],
                          preferred_element_type=F32)
            state_ref[g] = s_prev * cd_x[:, g * gw:(g + 1) * gw] + upd
        y_chunks.append(jnp.concatenate(y_parts, axis=1))
    y = jnp.concatenate(y_chunks, axis=0) + xs_all * dskip_ref[...]

    yg = y * _silu(uz_ref[...].astype(F32))
    outs = []
    for g in range(SSD_GROUPS):
        v = yg[:, g * gw:(g + 1) * gw]
        ms = jnp.mean(v * v, axis=-1, keepdims=True)
        outs.append(v * lax.rsqrt(ms + EPS))
    y_ssd = jnp.concatenate(outs, axis=1) * nw_ref[...]
    o_ref[:, CONV_GROUP_W:] = y_ssd.astype(BF16)


def _mix(u, dt_raw, conv_w, wxs, wbc, bxs, bbc, dtb, alog, dskip_x, nw):
    t = u.shape[0]
    tq = MIX_TILE
    gw = CONV_GROUP_W
    bcw = 2 * SSD_GROUPS * SSD_STATE
    const = lambda i: (0, 0)
    ublk = lambda k: pl.BlockSpec((tq, gw), lambda i, k=k: (i, k))
    full = lambda arr: pl.BlockSpec(arr.shape, const)
    return pl.pallas_call(
        _mix_kernel,
        grid=(t // tq,),
        in_specs=[ublk(0), ublk(1), ublk(2), ublk(3), ublk(4),
                  pl.BlockSpec((tq, bcw), lambda i: (i, 5 * gw // bcw)),
                  pl.BlockSpec((tq, LANES), lambda i: (i, 0)),
                  full(conv_w), full(wxs), full(wbc), full(bxs), full(bbc), full(dtb), full(alog),
                  full(dskip_x), full(nw)],
        out_specs=pl.BlockSpec((tq, 2 * gw), lambda i: (i, 0)),
        out_shape=jax.ShapeDtypeStruct((t, 2 * gw), BF16),
        scratch_shapes=[pltpu.VMEM((SUBLANES, gw), F32), pltpu.VMEM((SUBLANES, gw), F32),
                        pltpu.VMEM((SUBLANES, gw), F32), pltpu.VMEM((SUBLANES, bcw), F32),
                        pltpu.VMEM((SSD_GROUPS, SSD_STATE, gw // SSD_GROUPS), F32),
                        pltpu.VMEM((tq // SSD_CHUNK * SSD_HEADS, SSD_CHUNK, SSD_CHUNK), F32),
                        pltpu.VMEM((tq, gw), F32), pltpu.VMEM((tq, bcw), F32)],
        compiler_params=_cparams(("arbitrary",)),
        name="mix",
    )(u, u, u, u, u, u, dt_raw, conv_w, wxs, wbc, bxs, bbc, dtb, alog, dskip_x, nw)


def _outproj_kernel(y_ref, x_ref, w_ref, g1_ref, nw_ref, sc_ref, sh_ref, wrh_ref, br_ref,
                    x1_ref, h_ref, rt_ref, route_ref, acc_ref, hb_ref, hl_ref, *, rows):
    tm, d = x_ref.shape
    scale = nw_ref[...] * (1.0 + sc_ref[...])
    acc_ref[...] = jnp.dot(y_ref[...], w_ref[...], preferred_element_type=F32)
    for r0 in range(0, tm, rows):
        x1 = x_ref[r0:r0 + rows, :] + g1_ref[...] * acc_ref[r0:r0 + rows, :]
        x1_ref[r0:r0 + rows, :] = x1
        ms = jnp.mean(x1 * x1, axis=-1, keepdims=True)
        h = x1 * lax.rsqrt(ms + EPS) * scale + sh_ref[...]
        h_ref[r0:r0 + rows, :] = h
        hb = h.astype(BF16)
        hb_ref[r0:r0 + rows, :] = hb
        hl_ref[r0:r0 + rows, :] = (h - hb.astype(F32)).astype(BF16)

    both = lax.dot_general(wrh_ref[...], hb_ref[...], NT_DIMS, preferred_element_type=F32)
    lt = (both[:ROUTER_ROWS, :] + both[ROUTER_ROWS:, :]
          + lax.dot_general(wrh_ref[:ROUTER_ROWS, :], hl_ref[...], NT_DIMS, preferred_element_type=F32)
          + br_ref[...])
    sub = lax.broadcasted_iota(I32, (ROUTER_ROWS, tm), 0).astype(F32)
    ninf = jnp.float32(-jnp.inf)
    big = jnp.float32(1e9)
    is_g = (sub >= N_EXPERTS) & (sub < N_EXPERTS + N_EXPERT_GROUPS)
    gl = jnp.where(is_g, lt, ninf)
    gmax = jnp.max(gl, axis=0, keepdims=True)
    grp_p = 1.0 / jnp.sum(jnp.exp(gl - gmax), axis=0, keepdims=True)
    gidx = jnp.min(jnp.where(gl == gmax, sub - N_EXPERTS, big), axis=0, keepdims=True)
    lo = gidx * EXPERTS_PER_GROUP
    el = jnp.where((sub >= lo) & (sub < lo + EXPERTS_PER_GROUP), lt, ninf)
    m1 = jnp.max(el, axis=0, keepdims=True)
    i1 = jnp.min(jnp.where(el == m1, sub, big), axis=0, keepdims=True)
    el2 = jnp.where(sub == i1, ninf, el)
    m2 = jnp.max(el2, axis=0, keepdims=True)
    i2 = jnp.min(jnp.where(el2 == m2, sub, big), axis=0, keepdims=True)
    e = jnp.exp(m2 - m1)
    w1 = grp_p / (1.0 + e)
    w2 = grp_p * e / (1.0 + e)
    rt = jnp.where(sub == 0, i1, jnp.where(sub == 1, i2,
                   jnp.where(sub == 2, w1, jnp.where(sub == 3, w2, 0.0))))
    for c, r0 in enumerate(range(0, tm, rows)):
        rt_ref[c] = rt[:SUBLANES, r0:r0 + rows]
    route_ref[...] = jnp.concatenate([rt, jnp.zeros((LANES - ROUTER_ROWS, tm), F32)], axis=0).T


def _outproj(ycat, x, w_out, g1, nw, sc, sh, wr_hilo, b_r):
    t, d = x.shape
    tm = ROW_TILE
    rows = POS_CHUNK
    const = lambda i: (0, 0)
    row = pl.BlockSpec((1, d), const)
    tile = pl.BlockSpec((tm, d), lambda i: (i, 0))
    return pl.pallas_call(
        functools.partial(_outproj_kernel, rows=rows),
        grid=(t // tm,),
        in_specs=[tile, tile,
                  pl.BlockSpec((d, d), const, pipeline_mode=pl.Buffered(1)), row, row, row, row,
                  pl.BlockSpec((2 * ROUTER_ROWS, d), const),
                  pl.BlockSpec((ROUTER_ROWS, 1), const)],
        out_specs=[tile, tile,
                   pl.BlockSpec((tm // rows, SUBLANES, rows), lambda i: (i, 0, 0)),
                   pl.BlockSpec((tm, LANES), lambda i: (i, 0))],
        out_shape=[jax.ShapeDtypeStruct((t, d), F32), jax.ShapeDtypeStruct((t, d), F32),
                   jax.ShapeDtypeStruct((t // rows, SUBLANES, rows), F32),
                   jax.ShapeDtypeStruct((t, LANES), F32)],
        scratch_shapes=[pltpu.VMEM((tm, d), F32), pltpu.VMEM((tm, d), BF16), pltpu.VMEM((tm, d), BF16)],
        compiler_params=_cparams(("arbitrary",)),
        name="outproj",
    )(ycat, x, w_out, g1, nw, sc, sh, wr_hilo, b_r)


def _meta_kernel(rt_ref, pos_ref, meta_ref):
    nchunk, _, c = rt_ref.shape
    r = float(MOE_ROWS)
    sub = lax.broadcasted_iota(I32, (LANES, c), 0).astype(F32)

    def onehots(k):
        rt = rt_ref[k]
        return (sub == rt[0:1, :]).astype(F32), (sub == rt[1:2, :]).astype(F32)

    def count(k, cnt):
        oh1, oh2 = onehots(k)
        return cnt + jnp.sum(oh1 + oh2, axis=1, keepdims=True)

    cnt = lax.fori_loop(0, nchunk, count, jnp.zeros((LANES, 1), F32))
    padded = jnp.floor((cnt + (r - 1.0)) * (1.0 / r)) * r
    er = lax.broadcasted_iota(I32, (LANES, LANES), 0)
    ec = lax.broadcasted_iota(I32, (LANES, LANES), 1)
    strict_lower = (ec < er).astype(F32)
    pstart = jnp.dot(strict_lower, jnp.broadcast_to(padded, (LANES, LANES)), precision=HIGHEST,
                     preferred_element_type=F32)[:, 0:1]
    tr = lax.broadcasted_iota(I32, (c, c), 0)
    tc = lax.broadcasted_iota(I32, (c, c), 1)
    before = (tr < tc).astype(BF16)

    def place(k, run):
        oh1, oh2 = onehots(k)
        both = oh1 + oh2
        prior = jnp.dot(both.astype(BF16), before, preferred_element_type=F32)
        slot = prior + (pstart + run)
        p1 = jnp.sum(oh1 * slot, axis=0, keepdims=True)
        p2 = jnp.sum(oh2 * slot, axis=0, keepdims=True)
        pos_ref[k] = jnp.concatenate([p1, p2], axis=0).astype(I32)
        return run + jnp.sum(both, axis=1, keepdims=True)

    lax.fori_loop(0, nchunk, place, jnp.zeros((LANES, 1), F32))

    nbl = META_LANES
    e_sub = lax.broadcasted_iota(I32, (LANES, nbl), 0).astype(F32)
    blk0 = lax.broadcasted_iota(I32, (LANES, nbl), 1).astype(F32) * r
    pend = pstart + padded
    bexp = jnp.sum(jnp.where((e_sub < N_EXPERTS) & (pend <= blk0), 1.0, 0.0), axis=0, keepdims=True)
    bexp = jnp.minimum(bexp, N_EXPERTS - 1.0)
    cend = jnp.sum(jnp.where(e_sub == bexp, pstart + cnt, 0.0), axis=0, keepdims=True)
    nvalid = jnp.clip(cend - blk0[0:1, :], 0.0, r)
    total = jnp.sum(padded, axis=0, keepdims=True)
    used = blk0[0:1, :] < total
    nvalid = jnp.where(used, nvalid, 0.0)
    pend_row = jnp.sum(jnp.where(er == ec, pend, 0.0), axis=0, keepdims=True)
    b_sub = lax.broadcasted_iota(I32, (nbl, LANES), 0).astype(F32) * r
    e_lane = lax.broadcasted_iota(I32, (nbl, LANES), 1)
    bexp_col = jnp.sum(jnp.where((e_lane < N_EXPERTS) & (pend_row <= b_sub), 1.0, 0.0),
                       axis=1, keepdims=True)
    bexp_col = jnp.minimum(bexp_col, N_EXPERTS - 1.0)
    seg_end = jnp.sum(jnp.where(e_sub == bexp, pend, 0.0), axis=0, keepdims=True)
    bp_sub = lax.broadcasted_iota(I32, (nbl, nbl), 0).astype(F32) * r
    nxt = jnp.sum(jnp.where(bp_sub == seg_end, bexp_col, 0.0), axis=0, keepdims=True)
    nxt = jnp.where(used & (seg_end < total), nxt, -1.0)
    rowsel = lax.broadcasted_iota(I32, (SUBLANES, nbl), 0)
    meta = jnp.where(rowsel == 0, bexp, jnp.where(rowsel == 1, nvalid, jnp.where(rowsel == 2, nxt, 0.0)))
    meta_ref[...] = meta.astype(I32)


def _meta(rt):
    nchunk, _, c = rt.shape
    return pl.pallas_call(
        _meta_kernel,
        out_shape=[jax.ShapeDtypeStruct((nchunk, 2, c), I32),
                   jax.ShapeDtypeStruct((SUBLANES, META_LANES), I32)],
        compiler_params=pltpu.CompilerParams(vmem_limit_bytes=VMEM_LIMIT),
        name="meta",
    )(rt)


def _dispatch_kernel(meta_ref, pos_ref, h_ref, hs_hbm, zbuf, sem, zsem):
    i = pl.program_id(0)
    tm = h_ref.shape[0]
    nck, _, c = pos_ref.shape
    r = MOE_ROWS

    @pl.when(i == 0)
    def _():
        zbuf[...] = jnp.zeros_like(zbuf)

        def zero_copy(b):
            return pltpu.make_async_copy(zbuf, hs_hbm.at[pl.ds(pl.multiple_of(b * r, r), r)], zsem)

        def zstart(b, carry):
            @pl.when(meta_ref[META_NVALID + b] < r)
            def _():
                zero_copy(b).start()
            return carry

        def zwait(b, carry):
            @pl.when(meta_ref[META_NVALID + b] < r)
            def _():
                zero_copy(b).wait()
            return carry

        lax.fori_loop(0, hs_hbm.shape[0] // r, zstart, 0)
        lax.fori_loop(0, hs_hbm.shape[0] // r, zwait, 0)

    for ck in range(nck):
        for j in range(c):
            row = ck * c + j
            for k in range(2):
                pltpu.make_async_copy(h_ref.at[pl.ds(row, 1)],
                                      hs_hbm.at[pl.ds(pos_ref[ck, k, j], 1)], sem).start(priority=k)
    for k in range(2):
        pltpu.make_async_copy(h_ref, hs_hbm.at[pl.ds(0, tm)], sem).wait()


def _dispatch(meta_flat, pos, h, n_slots):
    t, d = h.shape
    tm = DISPATCH_TILE
    nck = tm // POS_CHUNK
    grid_spec = pltpu.PrefetchScalarGridSpec(
        num_scalar_prefetch=1,
        grid=(t // tm,),
        in_specs=[pl.BlockSpec((nck, 2, POS_CHUNK), lambda i, m: (i, 0, 0), memory_space=pltpu.SMEM),
                  pl.BlockSpec((tm, d), lambda i, m: (i, 0))],
        out_specs=pl.BlockSpec(memory_space=pl.ANY),
        scratch_shapes=[pltpu.VMEM((MOE_ROWS, d), F32), pltpu.SemaphoreType.DMA(()),
                        pltpu.SemaphoreType.DMA(())],
    )
    return pl.pallas_call(
        _dispatch_kernel,
        grid_spec=grid_spec,
        out_shape=jax.ShapeDtypeStruct((n_slots, d), F32),
        compiler_params=_cparams(("arbitrary",)),
        name="dispatch",
    )(meta_flat, pos, h)


def _moe_kernel(meta_ref, x_hbm, wg_hbm, wu_hbm, wd_hbm, y_ref, stg_g, stg_u, stg_d, wgb, wub, wdb, sems,
                xbuf, xsems):
    b = pl.program_id(0)
    nb = pl.num_programs(0)
    rows = xbuf.shape[1]

    def x_copy(j):
        s = lax.rem(j, 3)
        return pltpu.make_async_copy(x_hbm.at[pl.ds(pl.multiple_of(j * rows, rows), rows)],
                                     xbuf.at[s], xsems.at[s])

    def request(j):
        jc = jnp.minimum(j, nb - 1)

        @pl.when(jnp.logical_and(j < nb, meta_ref[META_NVALID + jc] > 0))
        def _():
            x_copy(jc).start()

    @pl.when(b == 0)
    def _():
        request(b)
        request(b + 1)

    request(b + 2)
    x_ref = xbuf.at[lax.rem(b, 3)]

    e = meta_ref[META_EXPERT + b]
    nxt = meta_ref[META_NEXT + b]
    used = meta_ref[META_NVALID + b] > 0

    @pl.when(used)
    def _():
        x_copy(b).wait()

    prev = meta_ref[META_EXPERT + jnp.maximum(b - 1, 0)]
    changed = jnp.logical_and(used, jnp.logical_or(b == 0, e != prev))
    d, f = wgb.shape
    kc = 512

    def copies(ex):
        return [pltpu.make_async_copy(w.at[ex], stg, sems.at[k])
                for k, (w, stg) in enumerate(((wg_hbm, stg_g), (wu_hbm, stg_u), (wd_hbm, stg_d)))]

    @pl.when(b == 0)
    def _():
        for cp in copies(e):
            cp.start()

    def convert_dot(lhs, stg, wb, n):
        acc = None
        for k0 in range(0, n, kc):
            w = stg[k0:k0 + kc, :].astype(BF16)
            wb[k0:k0 + kc, :] = w
            part = jnp.dot(lhs[:, k0:k0 + kc], w, preferred_element_type=F32)
            acc = part if acc is None else acc + part
        return acc

    @pl.when(changed)
    def _():
        cps = copies(e)
        x = x_ref[...].astype(BF16)
        cps[0].wait()
        gate = convert_dot(x, stg_g, wgb, d)
        cps[1].wait()
        up = convert_dot(x, stg_u, wub, d)
        hid = (_silu(gate) * up).astype(BF16)
        cps[2].wait()
        y_ref[...] = convert_dot(hid, stg_d, wdb, f)

        @pl.when(nxt >= 0)
        def _():
            for cp in copies(nxt):
                cp.start(priority=1)

    @pl.when(jnp.logical_and(used, jnp.logical_not(changed)))
    def _():
        x = x_ref[...].astype(BF16)
        gate = jnp.dot(x, wgb[...], preferred_element_type=F32)
        up = jnp.dot(x, wub[...], preferred_element_type=F32)
        hid = (_silu(gate) * up).astype(BF16)
        y_ref[...] = jnp.dot(hid, wdb[...], preferred_element_type=F32)

    @pl.when(jnp.logical_not(used))
    def _():
        y_ref[...] = jnp.zeros_like(y_ref)


def _moe(meta_flat, hs, w_gate, w_up, w_down):
    n_slots, d = hs.shape
    f = w_gate.shape[2]
    hbm = pl.BlockSpec(memory_space=pl.ANY)
    grid_spec = pltpu.PrefetchScalarGridSpec(
        num_scalar_prefetch=1,
        grid=(n_slots // MOE_ROWS,),
        in_specs=[hbm, hbm, hbm, hbm],
        out_specs=pl.BlockSpec((MOE_ROWS, d), lambda b, m: (b, 0)),
        scratch_shapes=[pltpu.VMEM((d, f), F32), pltpu.VMEM((d, f), F32), pltpu.VMEM((f, d), F32),
                        pltpu.VMEM((d, f), BF16), pltpu.VMEM((d, f), BF16), pltpu.VMEM((f, d), BF16),
                        pltpu.SemaphoreType.DMA((3,)),
                        pltpu.VMEM((3, MOE_ROWS, d), F32), pltpu.SemaphoreType.DMA((3,))],
    )
    return pl.pallas_call(
        _moe_kernel,
        grid_spec=grid_spec,
        out_shape=jax.ShapeDtypeStruct((n_slots, d), F32),
        compiler_params=_cparams(("arbitrary",)),
        name="moe",
    )(meta_flat, hs, w_gate, w_up, w_down)


def _final_kernel(pos_ref, pos1_ref, pos2_ref, y_hbm, x1_ref, route_ref, g2_ref, nw_ref, o_ref, ybuf, sems):
    i = pl.program_id(0)
    nt = pl.num_programs(0)
    tm = x1_ref.shape[0]
    slot = lax.rem(i, 3)

    def gather(p_ref, s, rows):
        for j in rows:
            for k in range(2):
                pltpu.make_async_copy(y_hbm.at[pl.ds(p_ref[0, k, j], 1)],
                                      ybuf.at[s, k, pl.ds(j, 1)], sems.at[s]).start(priority=k)

    def drain(s):
        for k in range(2):
            pltpu.make_async_copy(y_hbm.at[pl.ds(0, tm)], ybuf.at[s, k], sems.at[s]).wait()

    @pl.when(i == 0)
    def _():
        gather(pos_ref, 0, range(tm))
        gather(pos1_ref, 1, range(tm))

    def step(s):
        drain(s)
        rc = tm // 8
        for r0 in range(0, tm, rc):
            rt = route_ref[r0:r0 + rc, :]
            moe = rt[:, 2:3] * ybuf[s, 0, r0:r0 + rc, :] + rt[:, 3:4] * ybuf[s, 1, r0:r0 + rc, :]
            x2 = x1_ref[r0:r0 + rc, :] + g2_ref[...] * moe
            ms = jnp.mean(x2 * x2, axis=-1, keepdims=True)
            o_ref[r0:r0 + rc, :] = x2 * lax.rsqrt(ms + EPS) * nw_ref[...]
            gather(pos2_ref, (s + 2) % 3, range(r0, r0 + rc))

    for s in range(3):
        @pl.when(slot == s)
        def _(s=s):
            step(s)

    @pl.when(i == nt - 1)
    def _():
        drain(lax.rem(i + 1, 3))
        drain(lax.rem(i + 2, 3))


def _final(y_sorted, pos, x1, route, g2, nw):
    t, d = x1.shape
    tm = POS_CHUNK
    nt = t // tm
    const = lambda i: (0, 0)
    ahead = lambda k: pl.BlockSpec((1, 2, tm), lambda i: (jnp.minimum(i + k, nt - 1), 0, 0),
                                   memory_space=pltpu.SMEM)
    return pl.pallas_call(
        _final_kernel,
        grid=(nt,),
        in_specs=[ahead(0), ahead(1), ahead(2),
                  pl.BlockSpec(memory_space=pl.ANY),
                  pl.BlockSpec((tm, d), lambda i: (i, 0)),
                  pl.BlockSpec((tm, LANES), lambda i: (i, 0)),
                  pl.BlockSpec((1, d), const), pl.BlockSpec((1, d), const)],
        out_specs=pl.BlockSpec((tm, d), lambda i: (i, 0)),
        out_shape=jax.ShapeDtypeStruct((t, d), F32),
        scratch_shapes=[pltpu.VMEM((3, 2, tm, d), F32), pltpu.SemaphoreType.DMA((3,))],
        compiler_params=_cparams(("arbitrary",)),
        name="final",
    )(pos, pos, pos, y_sorted, x1, route, g2, nw)


def kernel(x, c, w_ada, b_ada, norm1_w, w_in, conv_w, ssd_conv_w, ssd_conv_b, dt_bias, a_log, d_skip,
           ssd_norm_w, w_out, norm2_w, w_router_grp, b_router_grp, w_router_exp, b_router_exp,
           w_gate, w_up, w_down, final_norm_w):
    bsz, seq, d = x.shape
    assert bsz == 1 and w_ada.shape[0] == 1
    t = bsz * seq
    assert t % DISPATCH_TILE == 0 and DISPATCH_TILE % ROW_TILE == 0
    xt = x.reshape(t, d)
    gw = CONV_GROUP_W
    n_main = 5 * gw + 2 * SSD_GROUPS * SSD_STATE
    n_heads = w_in.shape[2] - n_main
    assert n_heads == SSD_HEADS

    mod = _ada(c, w_ada[0], b_ada[0])
    sh1, sc1, g1, sh2, sc2, g2 = [mod[:, k * d:(k + 1) * d] for k in range(6)]

    w_all = w_in[0].astype(BF16)
    w_dt = jnp.pad(w_all[:, n_main:], ((0, 0), (0, LANES - n_heads)))
    u, dt_raw = _inproj(xt, norm1_w, sc1, sh1, w_all, w_dt, n_main)

    pad_h = lambda v: jnp.pad(v, ((0, 0), (0, LANES - n_heads)))
    ycat = _mix(u, dt_raw, conv_w[0], ssd_conv_w[0][:, :gw], ssd_conv_w[0][:, gw:],
                ssd_conv_b[:, :gw], ssd_conv_b[:, gw:], pad_h(dt_bias), pad_h(a_log),
                jnp.repeat(d_skip, SSD_HEAD_DIM, axis=1), ssd_norm_w)

    n_r = N_EXPERTS + N_EXPERT_GROUPS
    w_rt = jnp.pad(jnp.concatenate([w_router_exp[0], w_router_grp[0]], axis=1).T,
                   ((0, ROUTER_ROWS - n_r), (0, 0)))
    wr_hi = w_rt.astype(BF16)
    wr_lo = (w_rt - wr_hi.astype(F32)).astype(BF16)
    b_r = jnp.pad(jnp.concatenate([b_router_exp, b_router_grp], axis=1),
                  ((0, 0), (0, ROUTER_ROWS - n_r))).reshape(ROUTER_ROWS, 1)
    x1, h2, rt, route = _outproj(ycat, xt, w_out[0].astype(BF16), g1, norm2_w, sc2, sh2,
                                 jnp.concatenate([wr_hi, wr_lo], axis=0), b_r)

    pos, meta = _meta(rt)
    meta_flat = meta.reshape(SUBLANES * META_LANES)
    n_blocks = (2 * t + N_EXPERTS * (MOE_ROWS - 1) + MOE_ROWS - 1) // MOE_ROWS
    assert n_blocks <= META_LANES
    hs = _dispatch(meta_flat, pos, h2, n_blocks * MOE_ROWS)
    y_sorted = _moe(meta_flat, hs, w_gate[0], w_up[0], w_down[0])
    out = _final(y_sorted, pos, x1, route, g2, final_norm_w.reshape(1, d))
    return out.reshape(bsz, seq, d)
```
